```python
import jax, jax.numpy as jnp
from jax import lax
import numpy as np

D_MODEL = 1024
BATCH = 32
SEQ = 2048
DEPTH = 1

D_PLE = 256
D_MIX = 2 * D_MODEL
GM_WIDTH = D_MIX // 2
GM_HEADS = 8
GM_HEAD_DIM = GM_WIDTH // GM_HEADS
GM_CHUNK = 128
SSD_WIDTH = D_MIX - GM_WIDTH
SSD_HEAD_DIM = 64
SSD_HEADS = SSD_WIDTH // SSD_HEAD_DIM
SSD_GROUPS = 2
SSD_HEADS_PER_GROUP = SSD_HEADS // SSD_GROUPS
SSD_STATE = 128
SSD_CONV = 4
SSD_CHUNK = 128
D_FF = 4 * D_MODEL
EPS = 1e-6
GM_COLS = 2 * GM_WIDTH
SSD_CONV_CH = SSD_WIDTH + 2 * SSD_GROUPS * SSD_STATE
SSD_COLS = SSD_WIDTH + SSD_CONV_CH + SSD_HEADS
D_IN_PROJ = GM_COLS + SSD_COLS

kernel_name = "hybrid_gmlp_ssd_parallel_heads"


def rmsnorm(x, g):
    xf = x.astype(jnp.float32)
    y = xf * lax.rsqrt(jnp.mean(xf * xf, axis=-1, keepdims=True) + EPS)
    return (y * g.astype(jnp.float32)).astype(x.dtype)


def gmlp_chunk_mixer(uv, v_norm_g, ws, bs, out_norm_g):
    b, s, _ = uv.shape
    nc = s // GM_CHUNK
    uv = jax.nn.gelu(uv, approximate=False)
    u = uv[..., :GM_WIDTH].reshape(b, nc, GM_CHUNK, GM_HEADS, GM_HEAD_DIM)
    v = uv[..., GM_WIDTH:].reshape(b, s, GM_HEADS, GM_HEAD_DIM)
    v = rmsnorm(v, v_norm_g.reshape(GM_HEADS, GM_HEAD_DIM))
    v = v.reshape(b, nc, GM_CHUNK, GM_HEADS, GM_HEAD_DIM)
    causal = jnp.tril(jnp.ones((GM_CHUNK, GM_CHUNK), dtype=bool))
    w = jnp.where(causal[None], ws, jnp.zeros_like(ws))
    mixed = jnp.einsum('hts,bcshp->bcthp', w, v) + bs.T[None, None, :, :, None]
    y = (u * mixed).reshape(b, s, GM_WIDTH)
    return rmsnorm(y, out_norm_g)


def causal_depthwise_conv(x, w, bias):
    out = lax.conv_general_dilated(
        x, w[:, None, :], window_strides=(1,), padding=[(SSD_CONV - 1, 0)],
        dimension_numbers=('NWC', 'WIO', 'NWC'), feature_group_count=x.shape[-1])
    return out + bias


def ssd_chunked_scan(x, dt, A, B, C):
    b, s, G, R, P = x.shape
    N = B.shape[-1]
    nc, L = s // SSD_CHUNK, SSD_CHUNK
    x = x.reshape(b, nc, L, G, R, P)
    dt = dt.reshape(b, nc, L, G, R)
    B = B.reshape(b, nc, L, G, N)
    C = C.reshape(b, nc, L, G, N)
    cs = jnp.cumsum(dt * A, axis=2)
    x_dt = x * dt[..., None].astype(x.dtype)
    seg = cs[:, :, :, None] - cs[:, :, None, :]
    causal = jnp.tril(jnp.ones((L, L), dtype=bool))[:, :, None, None]
    lmat = jnp.exp(jnp.where(causal, seg, -jnp.inf)).astype(x.dtype)
    cb = jnp.einsum('bclgn,bcsgn->bclsg', C, B)
    y_diag = jnp.einsum('bclsgr,bcsgrp->bclgrp', cb[..., None] * lmat, x_dt)
    decay_states = jnp.exp(cs[:, :, -1:] - cs).astype(x.dtype)
    states = jnp.einsum('bclgn,bclgr,bclgrp->bcgrpn', B, decay_states, x_dt)
    chunk_decay = jnp.exp(cs[:, :, -1]).astype(x.dtype)

    def step(carry, inp):
        st, dec = inp
        return carry * dec[..., None, None] + st, carry

    init = jnp.zeros((b, G, R, P, N), dtype=states.dtype)
    _, prev = lax.scan(step, init, (jnp.moveaxis(states, 1, 0), jnp.moveaxis(chunk_decay, 1, 0)))
    prev = jnp.moveaxis(prev, 0, 1)
    y_off = jnp.einsum('bclgn,bcgrpn,bclgr->bclgrp', C, prev, jnp.exp(cs).astype(x.dtype))
    return (y_diag + y_off).reshape(b, s, G, R, P)


def ssd_mixer(zxbcdt, conv_w, conv_b, dt_bias, a_log, d_skip, norm_g):
    b, s, _ = zxbcdt.shape
    z = zxbcdt[..., :SSD_WIDTH]
    xbc = zxbcdt[..., SSD_WIDTH:SSD_WIDTH + SSD_CONV_CH]
    dt_raw = zxbcdt[..., SSD_WIDTH + SSD_CONV_CH:]
    xbc = jax.nn.silu(causal_depthwise_conv(xbc, conv_w, conv_b))
    xs = xbc[..., :SSD_WIDTH].reshape(b, s, SSD_GROUPS, SSD_HEADS_PER_GROUP, SSD_HEAD_DIM)
    Bm = xbc[..., SSD_WIDTH:SSD_WIDTH + SSD_GROUPS * SSD_STATE].reshape(b, s, SSD_GROUPS, SSD_STATE)
    Cm = xbc[..., SSD_WIDTH + SSD_GROUPS * SSD_STATE:].reshape(b, s, SSD_GROUPS, SSD_STATE)
    dt = jax.nn.softplus(dt_raw.astype(jnp.float32) + dt_bias.astype(jnp.float32))
    dt = dt.reshape(b, s, SSD_GROUPS, SSD_HEADS_PER_GROUP)
    A = -jnp.exp(a_log.astype(jnp.float32)).reshape(SSD_GROUPS, SSD_HEADS_PER_GROUP)
    y = ssd_chunked_scan(xs, dt, A, Bm, Cm)
    y = y + d_skip.reshape(SSD_GROUPS, SSD_HEADS_PER_GROUP)[:, :, None] * xs
    y = y.reshape(b, s, SSD_WIDTH) * jax.nn.silu(z)
    gw = SSD_WIDTH // SSD_GROUPS
    y = rmsnorm(y.reshape(b, s, SSD_GROUPS, gw), norm_g.reshape(SSD_GROUPS, gw))
    return y.reshape(b, s, SSD_WIDTH)


def setup_inputs(seed: int = 0) -> dict:
    key = jax.random.key(seed)
    ks = jax.random.split(key, 24)
    f32 = jnp.float32
    nrm = lambda k, shape, scale: jax.random.normal(k, shape, f32) * scale
    gain = lambda k, shape: 1.0 + 0.05 * jax.random.normal(k, shape, f32)
    dt_init = jnp.exp(jax.random.uniform(ks[14], (DEPTH, SSD_HEADS), f32,
                                         np.log(1e-3).astype(np.float32), np.log(1e-1).astype(np.float32)))
    return {
        "x": jax.random.normal(ks[0], (BATCH, SEQ, D_MODEL), f32),
        "p": jax.random.normal(ks[1], (DEPTH, BATCH, SEQ, D_PLE), f32),
        "norm_mix_g": gain(ks[2], (DEPTH, D_MODEL)),
        "w_in": nrm(ks[3], (DEPTH, D_MODEL, D_IN_PROJ), D_MODEL ** -0.5),
        "gm_v_norm_g": gain(ks[4], (DEPTH, GM_WIDTH)),
        "gm_ws": nrm(ks[5], (DEPTH, GM_HEADS, GM_CHUNK, GM_CHUNK), GM_CHUNK ** -0.5),
        "gm_bs": gain(ks[6], (DEPTH, GM_HEADS, GM_CHUNK)),
        "gm_out_norm_g": gain(ks[7], (DEPTH, GM_WIDTH)),
        "ssd_conv_w": nrm(ks[8], (DEPTH, SSD_CONV, SSD_CONV_CH), SSD_CONV ** -0.5),
        "ssd_conv_b": nrm(ks[9], (DEPTH, SSD_CONV_CH), 0.02),
        "ssd_dt_bias": dt_init + jnp.log(-jnp.expm1(-dt_init)),
        "ssd_a_log": jnp.log(jax.random.uniform(ks[10], (DEPTH, SSD_HEADS), f32, 1.0, 16.0)),
        "ssd_d": gain(ks[11], (DEPTH, SSD_HEADS)),
        "ssd_norm_g": gain(ks[12], (DEPTH, SSD_WIDTH)),
        "w_out": nrm(ks[13], (DEPTH, D_MIX, D_MODEL), D_MIX ** -0.5),
        "norm_mlp_g": gain(ks[15], (DEPTH, D_MODEL)),
        "w_ff1": nrm(ks[16], (DEPTH, D_MODEL, D_FF), D_MODEL ** -0.5),
        "w_ff2": nrm(ks[17], (DEPTH, D_FF, D_MODEL), D_FF ** -0.5),
        "ple_norm_g": gain(ks[18], (DEPTH, D_MODEL)),
        "w_ple_gate": nrm(ks[19], (DEPTH, D_MODEL, D_MODEL), D_MODEL ** -0.5),
        "w_ple_proj": nrm(ks[20], (DEPTH, D_PLE, D_MODEL), D_PLE ** -0.5),
        "final_norm_g": gain(ks[21], (D_MODEL,)),
    }


def reference(x, p, norm_mix_g, w_in, gm_v_norm_g, gm_ws, gm_bs, gm_out_norm_g,
              ssd_conv_w, ssd_conv_b, ssd_dt_bias, ssd_a_log, ssd_d, ssd_norm_g,
              w_out, norm_mlp_g, w_ff1, w_ff2, ple_norm_g, w_ple_gate, w_ple_proj,
              final_norm_g):
    h = x
    for i in range(DEPTH):
        proj = rmsnorm(h, norm_mix_g[i]) @ w_in[i]
        y_a = gmlp_chunk_mixer(proj[..., :GM_COLS], gm_v_norm_g[i], gm_ws[i], gm_bs[i],
                               gm_out_norm_g[i])
        y_b = ssd_mixer(proj[..., GM_COLS:], ssd_conv_w[i], ssd_conv_b[i], ssd_dt_bias[i],
                        ssd_a_log[i], ssd_d[i], ssd_norm_g[i])
        h = h + jnp.concatenate([y_a, y_b], axis=-1) @ w_out[i]
        hid = jax.nn.relu(rmsnorm(h, norm_mlp_g[i]) @ w_ff1[i])
        h = h + (hid * hid) @ w_ff2[i]
        gate = jax.nn.sigmoid(rmsnorm(h, ple_norm_g[i]) @ w_ple_gate[i])
        h = h + gate * (p[i] @ w_ple_proj[i])
    return rmsnorm(h, final_norm_g)
```

```python
import functools

import numpy as np
import jax
import jax.numpy as jnp
from jax import lax
from jax.experimental import pallas as pl
from jax.experimental.pallas import tpu as pltpu

D_MODEL = 1024
D_PLE = 256
D_FF = 4096
GM_WIDTH = 1024
GM_HEADS = 8
SSD_WIDTH = 1024
SSD_HEADS = 16
SSD_HEAD_DIM = 64
SSD_GROUPS = 2
SSD_STATE = 128
SSD_CONV = 4
SSD_CONV_CH = SSD_WIDTH + 2 * SSD_GROUPS * SSD_STATE
CHUNK = 128
EPS = 1e-6

LANES = 128
SUBLANES = 8
SEQ_TILE = 256
FFN_TILE = 512
FF_BLOCK = 1024
VMEM_LIMIT = 48 * 1024 * 1024

F32 = jnp.float32
BF16 = jnp.bfloat16


def _dot(a, b):
    return jnp.dot(a, b, preferred_element_type=F32)


def _rms(x, g):
    ms = jnp.mean(x * x, axis=-1, keepdims=True)
    return x * lax.rsqrt(ms + EPS) * g


def _silu(x):
    return x * jax.nn.sigmoid(x)


def _split3(a):
    hi = a.astype(BF16).astype(F32)
    r = a - hi
    mid = r.astype(BF16).astype(F32)
    return hi, mid, r - mid


def _expander_matrix():
    e = np.zeros((LANES, SSD_HEADS * LANES + SSD_WIDTH), np.float32)
    for piece in range(3):
        for r in range(SSD_HEADS):
            e[piece * 16 + r, r * LANES:(r + 1) * LANES] = 1.0
            e[48 + piece * 16 + r, SSD_HEADS * LANES + r * SSD_HEAD_DIM:SSD_HEADS * LANES + (r + 1) * SSD_HEAD_DIM] = 1.0
    return e


def _mixer_kernel(x_ref, gmix_ref, wuv_ref, wz_ref, wxbc_ref, wdt_ref, gv_ref, ws_ref, bsb_ref, gouta_ref,
                  convw_ref, convb_ref, dtb_ref, alog_ref, dskip_ref, gssd_ref, ee_ref, wout_ref,
                  o_ref,
                  uv_s, z_s, xbc_s, dt_s, wm_s, y_s, state_s):
    ts = x_ref.shape[0]

    @pl.when(pl.program_id(1) == 0)
    def _():
        state_s[...] = jnp.zeros_like(state_s)
        xbc_s[0:SUBLANES, :] = jnp.zeros((SUBLANES, SSD_CONV_CH), F32)

    row = lax.broadcasted_iota(jnp.int32, (CHUNK, CHUNK), 0)
    lane = lax.broadcasted_iota(jnp.int32, (CHUNK, CHUNK), 1)
    tril = row >= lane
    lo_half = lane < SSD_HEAD_DIM

    x = x_ref[...]
    xn = _rms(x, gmix_ref[...]).astype(BF16)
    uv = _dot(xn, wuv_ref[...])
    uv_s[...] = 0.5 * uv * (1.0 + lax.erf(uv * np.float32(np.sqrt(0.5))))
    z_s[...] = _dot(xn, wz_ref[...])
    xbc_s[SUBLANES:SUBLANES + ts, :] = _dot(xn, wxbc_ref[...])
    dt_raw = _dot(xn, wdt_ref[...]) + dtb_ref[...]
    dt = jnp.maximum(dt_raw, 0.0) + jnp.log1p(jnp.exp(-jnp.abs(dt_raw)))
    head_lane = lax.broadcasted_iota(jnp.int32, (ts, LANES), 1) < SSD_HEADS
    dt_s[...] = jnp.where(head_lane, dt, 0.0)

    for h in range(GM_HEADS):
        wm_s[h] = jnp.where(tril, ws_ref[h], 0.0).astype(BF16)

    a_neg = -jnp.exp(alog_ref[...])

    for c in range(ts // CHUNK):
        r0 = c * CHUNK
        rows = slice(r0, r0 + CHUNK)

        ya = []
        for h in range(GM_HEADS):
            cols = slice(h * LANES, (h + 1) * LANES)
            vh = uv_s[rows, GM_WIDTH + h * LANES:GM_WIDTH + (h + 1) * LANES]
            vn = _rms(vh, gv_ref[:, cols]).astype(BF16)
            mixed = _dot(wm_s[h], vn) + bsb_ref[h]
            ya.append(uv_s[rows, cols] * mixed)
        ya = _rms(jnp.concatenate(ya, axis=1), gouta_ref[...])
        y_s[rows, 0:GM_WIDTH] = ya.astype(BF16)

        acc = convb_ref[...] + convw_ref[0:1, :] * xbc_s[r0 + 5:r0 + 5 + CHUNK, :]
        for k in range(1, SSD_CONV):
            acc = acc + convw_ref[k:k + 1, :] * xbc_s[r0 + 5 + k:r0 + 5 + k + CHUNK, :]
        xbc = _silu(acc)
        xs = xbc[:, :SSD_WIDTH]
        bm = xbc[:, SSD_WIDTH:SSD_WIDTH + SSD_GROUPS * SSD_STATE]
        cm = xbc[:, SSD_WIDTH + SSD_GROUPS * SSD_STATE:]

        dt_c = dt_s[rows, :]
        dt_t = dt_c.T[0:SSD_HEADS, :]
        cs_t = dt_t * a_neg
        lane16 = lax.broadcasted_iota(jnp.int32, (SSD_HEADS, CHUNK), 1)
        k = 1
        while k < CHUNK:
            cs_t = cs_t + jnp.where(lane16 >= k, pltpu.roll(cs_t, k, axis=1), 0.0)
            k *= 2
        cs = jnp.concatenate([cs_t, jnp.zeros((CHUNK - SSD_HEADS, CHUNK), F32)], axis=0).T

        c_hi, c_mid, c_lo = _split3(cs)
        d_hi, d_mid, d_lo = _split3(dt_c)
        packed = (c_hi + pltpu.roll(c_mid, 16, axis=1) + pltpu.roll(c_lo, 32, axis=1)
                  + pltpu.roll(d_hi, 48, axis=1) + pltpu.roll(d_mid, 64, axis=1) + pltpu.roll(d_lo, 80, axis=1))
        bc = _dot(packed.astype(BF16), ee_ref[...])
        dtg = bc[:, SSD_HEADS * LANES:]

        cb = []
        for g in range(SSD_GROUPS):
            gs = slice(g * SSD_STATE, (g + 1) * SSD_STATE)
            cb.append(lax.dot_general(cm[:, gs].astype(BF16), bm[:, gs].astype(BF16),
                                      (((1,), (1,)), ((), ())), preferred_element_type=F32))

        y_pairs, csg = [], []
        for j in range(SSD_HEADS // 2):
            g = j // (SSD_HEADS // 2 // SSD_GROUPS)
            cg = cm[:, g * SSD_STATE:(g + 1) * SSD_STATE]
            lhs = []
            ces = []
            for r in (2 * j, 2 * j + 1):
                cs_lb = bc[:, r * LANES:(r + 1) * LANES]
                seg = cs_lb - cs_t[r:r + 1, :]
                lmat = jnp.exp(jnp.where(tril, seg, -jnp.inf))
                lhs.append((cb[g] * lmat * dt_t[r:r + 1, :]).astype(BF16))
                ces.append((cg * jnp.exp(cs_lb)).astype(BF16))
            pc = slice(j * LANES, (j + 1) * LANES)
            xp = xs[:, pc]
            sp = state_s[:, pc]
            rhs = jnp.concatenate([jnp.where(lo_half, xp, 0.0), jnp.where(lo_half, 0.0, xp),
                                   jnp.where(lo_half, sp, 0.0), jnp.where(lo_half, 0.0, sp)], axis=0).astype(BF16)
            y_pairs.append(_dot(jnp.concatenate(lhs + ces, axis=1), rhs))
            csg.append(jnp.where(lo_half, bc[:, 2 * j * LANES:(2 * j + 1) * LANES],
                                 bc[:, (2 * j + 1) * LANES:(2 * j + 2) * LANES]))
        csg = jnp.concatenate(csg, axis=1)
        cs_last = csg[CHUNK - 1:CHUNK, :]
        xd = (xs * (jnp.exp(cs_last - csg) * dtg)).astype(BF16)
        chunk_decay = jnp.exp(cs_last)
        for g in range(SSD_GROUPS):
            gc = slice(g * (SSD_WIDTH // SSD_GROUPS), (g + 1) * (SSD_WIDTH // SSD_GROUPS))
            b_t = bm[:, g * SSD_STATE:(g + 1) * SSD_STATE].T.astype(BF16)
            state_s[:, gc] = state_s[:, gc] * chunk_decay[:, gc] + _dot(b_t, xd[:, gc])

        y = jnp.concatenate(y_pairs, axis=1) + dskip_ref[...] * xs
        y = y * _silu(z_s[rows, :])
        gw = SSD_WIDTH // SSD_GROUPS
        yb = jnp.concatenate([_rms(y[:, g * gw:(g + 1) * gw], gssd_ref[:, g * gw:(g + 1) * gw])
                              for g in range(SSD_GROUPS)], axis=1)
        y_s[rows, GM_WIDTH:] = yb.astype(BF16)

    xbc_s[0:SUBLANES, :] = xbc_s[ts:ts + SUBLANES, :]
    o_ref[...] = x + _dot(y_s[...], wout_ref[...])


def _ffn_kernel(h_ref, p_ref, gmlp_ref, w1_ref, w2_ref, gple_ref, wgate_ref, wple_ref, gfin_ref, o_ref):
    h = h_ref[...]
    n = _rms(h, gmlp_ref[...]).astype(BF16)
    for j in range(D_FF // FF_BLOCK):
        blk = slice(j * FF_BLOCK, (j + 1) * FF_BLOCK)
        hid = jnp.maximum(_dot(n, w1_ref[:, blk]), 0.0)
        h = h + _dot((hid * hid).astype(BF16), w2_ref[blk, :])
    gate = jax.nn.sigmoid(_dot(_rms(h, gple_ref[...]).astype(BF16), wgate_ref[...]))
    h = h + gate * _dot(p_ref[...].astype(BF16), wple_ref[...])
    o_ref[...] = _rms(h, gfin_ref[...])


def _const_spec(shape):
    nd = len(shape)
    return pl.BlockSpec(shape, lambda *_: (0,) * nd, pipeline_mode=pl.Buffered(1))


def kernel(x, p, norm_mix_g, w_in, gm_v_norm_g, gm_ws, gm_bs, gm_out_norm_g, ssd_conv_w, ssd_conv_b, ssd_dt_bias,
           ssd_a_log, ssd_d, ssd_norm_g, w_out, norm_mlp_g, w_ff1, w_ff2, ple_norm_g, w_ple_gate, w_ple_proj,
           final_norm_g):
    b, s, d = x.shape
    depth = w_in.shape[0]
    assert d == D_MODEL and s % SEQ_TILE == 0 and (b * s) % FFN_TILE == 0
    ee = jnp.asarray(_expander_matrix(), BF16)
    row2 = lambda v: v.reshape(1, -1).astype(F32)
    h = x
    for i in range(depth):
        w = w_in[i].astype(BF16)
        c0, c1, c2 = 2 * GM_WIDTH, 2 * GM_WIDTH + SSD_WIDTH, 2 * GM_WIDTH + SSD_WIDTH + SSD_CONV_CH
        w_dt = jnp.pad(w[:, c2:], ((0, 0), (0, LANES - SSD_HEADS)))
        dt_bias = jnp.pad(ssd_dt_bias[i].astype(F32), (0, LANES - SSD_HEADS)).reshape(1, LANES)
        a_log = jnp.broadcast_to(ssd_a_log[i].astype(F32)[:, None], (SSD_HEADS, LANES))
        bsb = jnp.broadcast_to(gm_bs[i].astype(F32)[:, :, None], (GM_HEADS, CHUNK, LANES))
        d_skip = jnp.repeat(ssd_d[i].astype(F32), SSD_HEAD_DIM).reshape(1, SSD_WIDTH)
        mixer_in = [
            row2(norm_mix_g[i]), w[:, :c0], w[:, c0:c1], w[:, c1:c2], w_dt, row2(gm_v_norm_g[i]),
            gm_ws[i].astype(F32), bsb, row2(gm_out_norm_g[i]), ssd_conv_w[i].astype(F32), row2(ssd_conv_b[i]),
            dt_bias, a_log, d_skip, row2(ssd_norm_g[i]), ee, w_out[i].astype(BF16),
        ]
        h = pl.pallas_call(
            _mixer_kernel,
            name="mixer",
            grid=(b, s // SEQ_TILE),
            in_specs=[pl.BlockSpec((None, SEQ_TILE, D_MODEL), lambda bi, si: (bi, si, 0))]
                     + [_const_spec(a.shape) for a in mixer_in],
            out_specs=pl.BlockSpec((None, SEQ_TILE, D_MODEL), lambda bi, si: (bi, si, 0)),
            out_shape=jax.ShapeDtypeStruct((b, s, D_MODEL), F32),
            scratch_shapes=[
                pltpu.VMEM((SEQ_TILE, 2 * GM_WIDTH), F32),
                pltpu.VMEM((SEQ_TILE, SSD_WIDTH), F32),
                pltpu.VMEM((SEQ_TILE + SUBLANES, SSD_CONV_CH), F32),
                pltpu.VMEM((SEQ_TILE, LANES), F32),
                pltpu.VMEM((GM_HEADS, CHUNK, CHUNK), BF16),
                pltpu.VMEM((SEQ_TILE, GM_WIDTH + SSD_WIDTH), BF16),
                pltpu.VMEM((SSD_STATE, SSD_WIDTH), F32),
            ],
            compiler_params=pltpu.CompilerParams(dimension_semantics=("arbitrary", "arbitrary"),
                                                 vmem_limit_bytes=VMEM_LIMIT),
        )(h, *mixer_in)

        t = b * s
        ffn_in = [row2(norm_mlp_g[i]), w_ff1[i].astype(BF16), w_ff2[i].astype(BF16), row2(ple_norm_g[i]),
                  w_ple_gate[i].astype(BF16), w_ple_proj[i].astype(BF16),
                  row2(final_norm_g) if i == depth - 1 else None]
        assert depth == 1
        h = pl.pallas_call(
            _ffn_kernel,
            name="ffn",
            grid=(t // FFN_TILE,),
            in_specs=[pl.BlockSpec((FFN_TILE, D_MODEL), lambda ti: (ti, 0)),
                      pl.BlockSpec((FFN_TILE, D_PLE), lambda ti: (ti, 0))]
                     + [_const_spec(a.shape) for a in ffn_in],
            out_specs=pl.BlockSpec((FFN_TILE, D_MODEL), lambda ti: (ti, 0)),
            out_shape=jax.ShapeDtypeStruct((t, D_MODEL), F32),
            compiler_params=pltpu.CompilerParams(dimension_semantics=("arbitrary",),
                                                 vmem_limit_bytes=VMEM_LIMIT),
        )(h.reshape(t, D_MODEL), p[i].reshape(t, D_PLE), *ffn_in).reshape(b, s, D_MODEL)
    return h
```

```python
import functools

import numpy as np
import jax
import jax.numpy as jnp
from jax import lax
from jax.experimental import pallas as pl
from jax.experimental.pallas import tpu as pltpu

D_MODEL = 1024
D_PLE = 256
D_FF = 4096
GM_WIDTH = 1024
GM_HEADS = 8
SSD_WIDTH = 1024
SSD_HEADS = 16
SSD_HEAD_DIM = 64
SSD_GROUPS = 2
SSD_STATE = 128
SSD_CONV = 4
SSD_CONV_CH = SSD_WIDTH + 2 * SSD_GROUPS * SSD_STATE
CHUNK = 128
EPS = 1e-6

LANES = 128
SUBLANES = 8
SEQ_TILE = 256
FFN_TILE = 512
FF_BLOCK = 1024
MM_SLAB = 512
VMEM_LIMIT = 56 * 1024 * 1024

F32 = jnp.float32
BF16 = jnp.bfloat16


def _dot(a, b):
    return jnp.dot(a, b, preferred_element_type=F32)


def _rms(x, g):
    ms = jnp.mean(x * x, axis=-1, keepdims=True)
    return x * lax.rsqrt(ms + EPS) * g


def _silu(x):
    return x * jax.nn.sigmoid(x)


def _split3(a):
    hi = a.astype(BF16).astype(F32)
    r = a - hi
    mid = r.astype(BF16).astype(F32)
    return hi, mid, r - mid


def _expander_matrix():
    e = np.zeros((LANES, SSD_HEADS * LANES + SSD_WIDTH), np.float32)
    for piece in range(3):
        for r in range(SSD_HEADS):
            e[piece * 16 + r, r * LANES:(r + 1) * LANES] = 1.0
            e[48 + piece * 16 + r, SSD_HEADS * LANES + r * SSD_HEAD_DIM:SSD_HEADS * LANES + (r + 1) * SSD_HEAD_DIM] = 1.0
    return e


def _matmul_items(x_ref, xres_ref, o_ref, y_s, w, pbuf, xn_s):
    gmix_ref, wuv_ref, wz_ref, wxbc_ref, wdt_ref, dtb_ref, wout_ref = w
    uv_s, z_s, xbc_s, dt_s = pbuf
    ts = x_ref.shape[0]
    items = []

    def out_slab(cols):
        def run():
            o_ref[:, cols] = xres_ref[:, cols] + _dot(y_s[...], wout_ref[:, cols])
        return run

    def norm():
        xn_s[...] = _rms(x_ref[...], gmix_ref[...]).astype(BF16)

    def uv_slab(cols):
        def run():
            uv = _dot(xn_s[...], wuv_ref[:, cols])
            uv_s[:, cols] = 0.5 * uv * (1.0 + lax.erf(uv * np.float32(np.sqrt(0.5))))
        return run

    def plain_slab(dst, w_ref, cols, row0):
        def run():
            dst[row0:row0 + ts, cols] = _dot(xn_s[...], w_ref[:, cols])
        return run

    def dt_item():
        dt_raw = lax.dot_general(wdt_ref[...], xn_s[...], (((1,), (1,)), ((), ())),
                                 preferred_element_type=F32) + dtb_ref[...]
        dt_s[...] = jnp.maximum(dt_raw, 0.0) + jnp.log1p(jnp.exp(-jnp.abs(dt_raw)))

    slabs = lambda width: [slice(c, c + MM_SLAB) for c in range(0, width, MM_SLAB)]
    k_out = (GM_WIDTH + SSD_WIDTH) / D_MODEL
    items += [(k_out, out_slab(c)) for c in slabs(D_MODEL)]
    items.append((0.25, norm))
    items += [(1.0, uv_slab(c)) for c in slabs(2 * GM_WIDTH)]
    items += [(1.0, plain_slab(z_s, wz_ref, c, 0)) for c in slabs(SSD_WIDTH)]
    items += [(1.0, plain_slab(xbc_s, wxbc_ref, c, SUBLANES)) for c in slabs(SSD_CONV_CH)]
    items.append((0.5, dt_item))
    return items


SEG_COST = dict(gmlp_pair=1, gmlp_norm=3, conv=12, decay=1, head_pair=2, state=3, gate=6)
CHUNK_COST = (GM_HEADS // 2 * SEG_COST["gmlp_pair"] + SEG_COST["gmlp_norm"] + SEG_COST["conv"] + SEG_COST["decay"]
              + SSD_HEADS // 2 * SEG_COST["head_pair"] + SEG_COST["state"] + SEG_COST["gate"])


def _run_interleaved(main, main_total, items):
    side_total = sum(c for c, _ in items)
    done, side_done, main_done = 0, 0.0, 0.0
    for cost in main:
        main_done += cost
        while done < len(items) and (side_done + items[done][0] / 2) * main_total <= main_done * side_total:
            side_done += items[done][0]
            items[done][1]()
            done += 1
    for _, fn in items[done:]:
        fn()


def _process_stage(pbuf, w, wm_s, y_s, state_s, hist_s, first):
    gv_ref, bsb_ref, gouta_ref, convw_ref, convb_ref, alog_ref, dskip_ref, gssd_ref, ee_ref = w
    uv_s, z_s, xbc_s, dt_s = pbuf
    ts = uv_s.shape[0]
    carried = (lambda v: v) if first is None else (lambda v: jnp.where(first, 0.0, v))

    row = lax.broadcasted_iota(jnp.int32, (CHUNK, CHUNK), 0)
    lane = lax.broadcasted_iota(jnp.int32, (CHUNK, CHUNK), 1)
    tril = row >= lane
    lo_half = lane < SSD_HEAD_DIM
    a_neg = -jnp.exp(alog_ref[...])
    conv_lag = SUBLANES - (SSD_CONV - 1)

    xbc_s[0:SUBLANES, :] = carried(hist_s[...])

    for c in range(ts // CHUNK):
        r0 = c * CHUNK
        rows = slice(r0, r0 + CHUNK)
        keep = carried if c == 0 else (lambda v: v)

        ya = []
        for h in range(GM_HEADS):
            cols = slice(h * LANES, (h + 1) * LANES)
            vh = uv_s[rows, GM_WIDTH + h * LANES:GM_WIDTH + (h + 1) * LANES]
            vn = _rms(vh, gv_ref[:, cols]).astype(BF16)
            mixed = _dot(wm_s[h], vn) + bsb_ref[h]
            ya.append(uv_s[rows, cols] * mixed)
            if h % 2 == 1:
                yield SEG_COST["gmlp_pair"]
        ya = _rms(jnp.concatenate(ya, axis=1), gouta_ref[...])
        y_s[rows, 0:GM_WIDTH] = ya.astype(BF16)
        yield SEG_COST["gmlp_norm"]

        acc = convb_ref[...]
        for k in reversed(range(SSD_CONV)):
            acc = acc + convw_ref[k:k + 1, :] * xbc_s[r0 + conv_lag + k:r0 + conv_lag + k + CHUNK, :]
        xbc = _silu(acc)
        xs = xbc[:, :SSD_WIDTH]
        bm = xbc[:, SSD_WIDTH:SSD_WIDTH + SSD_GROUPS * SSD_STATE]
        cm = xbc[:, SSD_WIDTH + SSD_GROUPS * SSD_STATE:]
        yield SEG_COST["conv"]

        dt_t = dt_s[:, rows]
        cs_t = dt_t * a_neg
        lane16 = lax.broadcasted_iota(jnp.int32, (SSD_HEADS, CHUNK), 1)
        k = 1
        while k < CHUNK:
            cs_t = cs_t + jnp.where(lane16 >= k, pltpu.roll(cs_t, k, axis=1), 0.0)
            k *= 2

        pieces = list(_split3(cs_t)) + list(_split3(dt_t))
        pad = jnp.zeros((CHUNK - len(pieces) * SSD_HEADS, CHUNK), F32)
        packed = jnp.concatenate(pieces + [pad], axis=0).T
        bc = _dot(packed.astype(BF16), ee_ref[...])
        dtg = bc[:, SSD_HEADS * LANES:]

        cb = []
        for g in range(SSD_GROUPS):
            gs = slice(g * SSD_STATE, (g + 1) * SSD_STATE)
            cb.append(lax.dot_general(cm[:, gs].astype(BF16), bm[:, gs].astype(BF16),
                                      (((1,), (1,)), ((), ())), preferred_element_type=F32))
        yield SEG_COST["decay"]

        y_pairs, csg = [], []
        for j in range(SSD_HEADS // 2):
            g = j // (SSD_HEADS // 2 // SSD_GROUPS)
            cg = cm[:, g * SSD_STATE:(g + 1) * SSD_STATE]
            lhs = []
            ces = []
            for r in (2 * j, 2 * j + 1):
                cs_lb = bc[:, r * LANES:(r + 1) * LANES]
                seg = cs_lb - cs_t[r:r + 1, :]
                lmat = jnp.exp(jnp.where(tril, seg, -jnp.inf))
                lhs.append((cb[g] * lmat * dt_t[r:r + 1, :]).astype(BF16))
                ces.append((cg * jnp.exp(cs_lb)).astype(BF16))
            pc = slice(j * LANES, (j + 1) * LANES)
            xp = xs[:, pc]
            sp = keep(state_s[:, pc])
            rhs = jnp.concatenate([jnp.where(lo_half, xp, 0.0), jnp.where(lo_half, 0.0, xp),
                                   jnp.where(lo_half, sp, 0.0), jnp.where(lo_half, 0.0, sp)], axis=0).astype(BF16)
            y_pairs.append(_dot(jnp.concatenate(lhs + ces, axis=1), rhs))
            csg.append(jnp.where(lo_half, bc[:, 2 * j * LANES:(2 * j + 1) * LANES],
                                 bc[:, (2 * j + 1) * LANES:(2 * j + 2) * LANES]))
            yield SEG_COST["head_pair"]
        csg = jnp.concatenate(csg, axis=1)
        cs_last = csg[CHUNK - 1:CHUNK, :]
        xd = (xs * (jnp.exp(cs_last - csg) * dtg)).astype(BF16)
        chunk_decay = jnp.exp(cs_last)
        for g in range(SSD_GROUPS):
            gc = slice(g * (SSD_WIDTH // SSD_GROUPS), (g + 1) * (SSD_WIDTH // SSD_GROUPS))
            b_t = bm[:, g * SSD_STATE:(g + 1) * SSD_STATE].T.astype(BF16)
            state_s[:, gc] = keep(state_s[:, gc]) * chunk_decay[:, gc] + _dot(b_t, xd[:, gc])
        yield SEG_COST["state"]

        y = jnp.concatenate(y_pairs, axis=1) + dskip_ref[...] * xs
        y = y * _silu(z_s[rows, :])
        gw = SSD_WIDTH // SSD_GROUPS
        yb = jnp.concatenate([_rms(y[:, g * gw:(g + 1) * gw], gssd_ref[:, g * gw:(g + 1) * gw])
                              for g in range(SSD_GROUPS)], axis=1)
        y_s[rows, GM_WIDTH:] = yb.astype(BF16)
        yield SEG_COST["gate"]

    hist_s[...] = xbc_s[ts:ts + SUBLANES, :]


def _mixer_kernel(xa_ref, xc_ref, gmix_ref, wuv_ref, wz_ref, wxbc_ref, wdt_ref, gv_ref, ws_ref, bsb_ref, gouta_ref,
                  convw_ref, convb_ref, dtb_ref, alog_ref, dskip_ref, gssd_ref, ee_ref, wout_ref,
                  o_ref,
                  uv0, z0, xbc0, dt0, uv1, z1, xbc1, dt1, y0, y1, wm_s, state_s, hist_s, xn_s, *,
                  tiles_per_seq):
    j = pl.program_id(0)
    ts = SEQ_TILE
    p0, p1 = (uv0, z0, xbc0, dt0), (uv1, z1, xbc1, dt1)

    @pl.when(j == 0)
    def _():
        for ref in (uv1, z1, xbc1, dt1, y0, y1, state_s, hist_s):
            ref[...] = jnp.zeros_like(ref)
        row = lax.broadcasted_iota(jnp.int32, (CHUNK, CHUNK), 0)
        lane = lax.broadcasted_iota(jnp.int32, (CHUNK, CHUNK), 1)
        for h in range(GM_HEADS):
            wm_s[h] = jnp.where(row >= lane, ws_ref[h], 0.0).astype(BF16)

    w_proj = (gmix_ref, wuv_ref, wz_ref, wxbc_ref, wdt_ref, dtb_ref, wout_ref)
    w_proc = (gv_ref, bsb_ref, gouta_ref, convw_ref, convb_ref, alog_ref, dskip_ref, gssd_ref, ee_ref)
    first_even = (2 * j) % tiles_per_seq == 0

    n_yields = (ts // CHUNK) * CHUNK_COST
    lo, hi = slice(0, ts), slice(ts, 2 * ts)
    _run_interleaved(_process_stage(p1, w_proc, wm_s, y1, state_s, hist_s, None), n_yields,
                     _matmul_items(xa_ref.at[lo], xc_ref.at[lo], o_ref.at[lo], y0, w_proj, p0, xn_s))
    _run_interleaved(_process_stage(p0, w_proc, wm_s, y0, state_s, hist_s, first_even), n_yields,
                     _matmul_items(xa_ref.at[hi], xc_ref.at[hi], o_ref.at[hi], y1, w_proj, p1, xn_s))


def _ffn_kernel(h_ref, p_ref, gmlp_ref, w1_ref, w2_ref, gple_ref, wgate_ref, wple_ref, gfin_ref, o_ref):
    h = h_ref[...]
    n = _rms(h, gmlp_ref[...]).astype(BF16)
    for j in range(D_FF // FF_BLOCK):
        blk = slice(j * FF_BLOCK, (j + 1) * FF_BLOCK)
        hid = jnp.maximum(_dot(n, w1_ref[:, blk]), 0.0)
        h = h + _dot((hid * hid).astype(BF16), w2_ref[blk, :])
    gate = jax.nn.sigmoid(_dot(_rms(h, gple_ref[...]).astype(BF16), wgate_ref[...]))
    h = h + gate * _dot(p_ref[...].astype(BF16), wple_ref[...])
    o_ref[...] = _rms(h, gfin_ref[...])


def _const_spec(shape):
    nd = len(shape)
    return pl.BlockSpec(shape, lambda *_: (0,) * nd, pipeline_mode=pl.Buffered(1))


def kernel(x, p, norm_mix_g, w_in, gm_v_norm_g, gm_ws, gm_bs, gm_out_norm_g, ssd_conv_w, ssd_conv_b, ssd_dt_bias,
           ssd_a_log, ssd_d, ssd_norm_g, w_out, norm_mlp_g, w_ff1, w_ff2, ple_norm_g, w_ple_gate, w_ple_proj,
           final_norm_g):
    b, s, d = x.shape
    t = b * s
    depth = w_in.shape[0]
    tiles_per_seq = s // SEQ_TILE
    assert depth == 1 and d == D_MODEL and s % (2 * SEQ_TILE) == 0 and t % FFN_TILE == 0
    ee = jnp.asarray(_expander_matrix(), BF16)
    row2 = lambda v: v.reshape(1, -1).astype(F32)
    i = 0

    w = w_in[i].astype(BF16)
    c0, c1, c2 = 2 * GM_WIDTH, 2 * GM_WIDTH + SSD_WIDTH, 2 * GM_WIDTH + SSD_WIDTH + SSD_CONV_CH
    w_dt = w[:, c2:].T
    dt_bias = jnp.broadcast_to(ssd_dt_bias[i].astype(F32)[:, None], (SSD_HEADS, SEQ_TILE))
    a_log = jnp.broadcast_to(ssd_a_log[i].astype(F32)[:, None], (SSD_HEADS, LANES))
    bsb = jnp.broadcast_to(gm_bs[i].astype(F32)[:, :, None], (GM_HEADS, CHUNK, LANES))
    d_skip = jnp.repeat(ssd_d[i].astype(F32), SSD_HEAD_DIM).reshape(1, SSD_WIDTH)
    mixer_in = [
        row2(norm_mix_g[i]), w[:, :c0], w[:, c0:c1], w[:, c1:c2], w_dt, row2(gm_v_norm_g[i]),
        gm_ws[i].astype(F32), bsb, row2(gm_out_norm_g[i]), ssd_conv_w[i].astype(F32), row2(ssd_conv_b[i]),
        dt_bias, a_log, d_skip, row2(ssd_norm_g[i]), ee, w_out[i].astype(BF16),
    ]
    nb = t // (2 * SEQ_TILE)
    x2 = x.reshape(t, D_MODEL)
    pbufs = [
        pltpu.VMEM((SEQ_TILE, 2 * GM_WIDTH), F32),
        pltpu.VMEM((SEQ_TILE, SSD_WIDTH), F32),
        pltpu.VMEM((SEQ_TILE + SUBLANES, SSD_CONV_CH), F32),
        pltpu.VMEM((SSD_HEADS, SEQ_TILE), F32),
    ]
    h = pl.pallas_call(
        functools.partial(_mixer_kernel, tiles_per_seq=tiles_per_seq),
        name="mixer",
        grid=(nb + 1,),
        in_specs=[pl.BlockSpec((2 * SEQ_TILE, D_MODEL), lambda j: (jnp.minimum(j, nb - 1), 0)),
                  pl.BlockSpec((2 * SEQ_TILE, D_MODEL), lambda j: (jnp.maximum(j - 1, 0), 0))]
                 + [_const_spec(a.shape) for a in mixer_in],
        out_specs=pl.BlockSpec((2 * SEQ_TILE, D_MODEL), lambda j: (jnp.maximum(j - 1, 0), 0)),
        out_shape=jax.ShapeDtypeStruct((t, D_MODEL), F32),
        scratch_shapes=pbufs + pbufs + [
            pltpu.VMEM((SEQ_TILE, GM_WIDTH + SSD_WIDTH), BF16),
            pltpu.VMEM((SEQ_TILE, GM_WIDTH + SSD_WIDTH), BF16),
            pltpu.VMEM((GM_HEADS, CHUNK, CHUNK), BF16),
            pltpu.VMEM((SSD_STATE, SSD_WIDTH), F32),
            pltpu.VMEM((SUBLANES, SSD_CONV_CH), F32),
            pltpu.VMEM((SEQ_TILE, D_MODEL), BF16),
        ],
        compiler_params=pltpu.CompilerParams(dimension_semantics=("arbitrary",), vmem_limit_bytes=VMEM_LIMIT),
    )(x2, x2, *mixer_in)

    ffn_in = [row2(norm_mlp_g[i]), w_ff1[i].astype(BF16), w_ff2[i].astype(BF16), row2(ple_norm_g[i]),
              w_ple_gate[i].astype(BF16), w_ple_proj[i].astype(BF16), row2(final_norm_g)]
    out = pl.pallas_call(
        _ffn_kernel,
        name="ffn",
        grid=(t // FFN_TILE,),
        in_specs=[pl.BlockSpec((FFN_TILE, D_MODEL), lambda ti: (ti, 0)),
                  pl.BlockSpec((FFN_TILE, D_PLE), lambda ti: (ti, 0))]
                 + [_const_spec(a.shape) for a in ffn_in],
        out_specs=pl.BlockSpec((FFN_TILE, D_MODEL), lambda ti: (ti, 0)),
        out_shape=jax.ShapeDtypeStruct((t, D_MODEL), F32),
        compiler_params=pltpu.CompilerParams(dimension_semantics=("arbitrary",), vmem_limit_bytes=VMEM_LIMIT),
    )(h, p[i].reshape(t, D_PLE), *ffn_in)
    return out.reshape(b, s, D_MODEL)
```

```python
import functools

import numpy as np
import jax
import jax.numpy as jnp
from jax import lax
from jax.experimental import pallas as pl
from jax.experimental.pallas import tpu as pltpu

D_MODEL = 1024
D_PLE = 256
D_FF = 4096
GM_WIDTH = 1024
GM_HEADS = 8
SSD_WIDTH = 1024
SSD_HEADS = 16
SSD_HEAD_DIM = 64
SSD_GROUPS = 2
SSD_GROUP_WIDTH = SSD_WIDTH // SSD_GROUPS
SSD_STATE = 128
SSD_CONV = 4
SSD_CONV_CH = SSD_WIDTH + 2 * SSD_GROUPS * SSD_STATE
CHUNK = 128
EPS = 1e-6
LOG2E = float(np.log2(np.e))

LANES = 128
SUBLANES = 8
SEQ_TILE = 256
FFN_TILE = 512
FF_BLOCK = 1024
MM_SLAB = 512
OUT_SLAB = 512
CONV_SLAB = 512
LEAD = 0.0
VMEM_LIMIT = 56 * 1024 * 1024

F32 = jnp.float32
BF16 = jnp.bfloat16


def _dot(a, b):
    return jnp.dot(a, b, preferred_element_type=F32)


def _rms(x, g=None):
    ms = jnp.mean(x * x, axis=-1, keepdims=True)
    y = x * lax.rsqrt(ms + EPS)
    return y if g is None else y * g


def _silu(x):
    return x * jax.nn.sigmoid(x)


def _split3(a):
    hi = a.astype(BF16).astype(F32)
    r = a - hi
    mid = r.astype(BF16).astype(F32)
    return hi, mid, r - mid


def _expander_matrix():
    e = np.zeros((LANES, SSD_HEADS * LANES + SSD_WIDTH), np.float32)
    for piece in range(3):
        for r in range(SSD_HEADS):
            e[piece * 16 + r, r * LANES:(r + 1) * LANES] = 1.0
            e[48 + piece * 16 + r, SSD_HEADS * LANES + r * SSD_HEAD_DIM:SSD_HEADS * LANES + (r + 1) * SSD_HEAD_DIM] = 1.0
    return e


SEG_COST = dict(gmlp_pair=(130, 64), gmlp_norm=(190, 0), conv_slab=(230, 0), decay=(100, 450), yoff=(20, 128),
                head_pair=(60, 32), state=(150, 128), gate_half=(175, 0))
ITEM_COST = dict(out_slab=(160, 1024), norm=(280, 0), uv_slab=(250, 512), plain_slab=(64, 512), dt=(30, 300))


def _matmul_items(x_ref, xres_ref, o_ref, y_s, w, pbuf, xn_s):
    wuv_ref, wz_ref, wxbc_ref, wdt_ref, dtb_ref, wout_ref = w
    uv_s, z_s, xbc_s, dt_s = pbuf
    ts = x_ref.shape[0]

    def norm():
        xn_s[...] = _rms(x_ref[...]).astype(BF16)

    def uv_slab(cols):
        def run():
            uv = _dot(xn_s[...], wuv_ref[:, cols])
            uv_s[:, cols] = 0.5 * uv * (1.0 + lax.erf(uv * np.float32(np.sqrt(0.5))))
        return run

    def plain_slab(dst, w_ref, cols, row0):
        def run():
            dst[row0:row0 + ts, cols] = _dot(xn_s[...], w_ref[:, cols])
        return run

    def dt_item():
        dt_raw = lax.dot_general(wdt_ref[...], xn_s[...], (((1,), (1,)), ((), ())),
                                 preferred_element_type=F32) + dtb_ref[...]
        dt_s[...] = jnp.maximum(dt_raw, 0.0) + jnp.log1p(jnp.exp(-jnp.abs(dt_raw)))

    def out_slab(cols):
        def run():
            o_ref[:, cols] = xres_ref[:, cols] + _dot(y_s[...], wout_ref[:, cols])
        return run

    slabs = lambda width: [slice(c, c + MM_SLAB) for c in range(0, width, MM_SLAB)]
    items = [(ITEM_COST["norm"], norm)]
    items += [(ITEM_COST["uv_slab"], uv_slab(c)) for c in slabs(2 * GM_WIDTH)]
    items += [(ITEM_COST["plain_slab"], plain_slab(xbc_s, wxbc_ref, c, SUBLANES)) for c in slabs(SSD_CONV_CH)]
    items += [(ITEM_COST["plain_slab"], plain_slab(z_s, wz_ref, c, 0)) for c in slabs(SSD_WIDTH)]
    items.append((ITEM_COST["dt"], dt_item))
    items += [(ITEM_COST["out_slab"], out_slab(slice(c, c + OUT_SLAB))) for c in range(0, D_MODEL, OUT_SLAB)]
    return items


def _run_interleaved(main, main_total, items):
    side_total = sum(c[1] for c, _ in items)
    v_done = side_done = 0.0
    pending = list(items)
    for v, _ in main:
        v_done += v
        while pending and ((side_done + pending[0][0][1] / 2) * main_total[0]
                           <= (v_done + LEAD * main_total[0]) * side_total):
            (_, m_item), fn = pending.pop(0)
            fn()
            side_done += m_item
    for _, fn in pending:
        fn()


def _process_cost(n_chunks):
    per_chunk = ((SSD_CONV_CH // CONV_SLAB) * np.array(SEG_COST["conv_slab"]) + np.array(SEG_COST["decay"])
                 + np.array(SEG_COST["yoff"]) + (SSD_HEADS // 2) * np.array(SEG_COST["head_pair"])
                 + np.array(SEG_COST["state"]) + SSD_GROUPS * np.array(SEG_COST["gate_half"])
                 + np.array(SEG_COST["gmlp_norm"]))
    return tuple((GM_HEADS // 2) * np.array(SEG_COST["gmlp_pair"]) + n_chunks * per_chunk)


def _process_stage(pbuf, w, wm_s, y_s, state_s, hist_s, first):
    gv_ref, bsb_ref, convw_ref, convb_ref, alog_ref, dskip_ref, ee_ref = w
    uv_s, z_s, xbc_s, dt_s = pbuf
    ts = uv_s.shape[0]
    n_chunks = ts // CHUNK
    chunk_rows = [slice(c * CHUNK, (c + 1) * CHUNK) for c in range(n_chunks)]
    carried = (lambda v: v) if first is None else (lambda v: jnp.where(first, 0.0, v))

    row = lax.broadcasted_iota(jnp.int32, (CHUNK, CHUNK), 0)
    lane = lax.broadcasted_iota(jnp.int32, (CHUNK, CHUNK), 1)
    tril = row >= lane
    lo_half = lane < SSD_HEAD_DIM
    a_neg = -jnp.exp(alog_ref[...]) * LOG2E

    xbc_s[0:SUBLANES, :] = carried(hist_s[...])

    for h in range(GM_HEADS):
        cols = slice(h * LANES, (h + 1) * LANES)
        vcols = slice(GM_WIDTH + h * LANES, GM_WIDTH + (h + 1) * LANES)
        vn = jnp.concatenate([_rms(uv_s[rows, vcols], gv_ref[:, cols]).astype(BF16) for rows in chunk_rows], axis=1)
        mixed = _dot(wm_s[h], vn)
        for c, rows in enumerate(chunk_rows):
            uv_s[rows, cols] = uv_s[rows, cols] * (mixed[:, c * LANES:(c + 1) * LANES] + bsb_ref[h])
        if h % 2 == 1:
            yield SEG_COST["gmlp_pair"]

    for c, rows in enumerate(chunk_rows):
        r0 = c * CHUNK
        keep = carried if c == 0 else (lambda v: v)

        y_s[rows, 0:GM_WIDTH] = _rms(uv_s[rows, 0:GM_WIDTH]).astype(BF16)
        yield SEG_COST["gmlp_norm"]

        xbc = []
        for c0 in range(0, SSD_CONV_CH, CONV_SLAB):
            ccols = slice(c0, c0 + CONV_SLAB)
            blk = xbc_s[r0:r0 + SUBLANES + CHUNK, ccols]
            acc = convb_ref[:, ccols] + convw_ref[SSD_CONV - 1:SSD_CONV, ccols] * blk[SUBLANES:, :]
            for lag in range(1, SSD_CONV):
                k = SSD_CONV - 1 - lag
                acc = acc + convw_ref[k:k + 1, ccols] * pltpu.roll(blk, lag, axis=0)[SUBLANES:, :]
            xbc.append(_silu(acc))
            yield SEG_COST["conv_slab"]
        xbc = jnp.concatenate(xbc, axis=1)
        xs = xbc[:, :SSD_WIDTH]
        bm = xbc[:, SSD_WIDTH:SSD_WIDTH + SSD_GROUPS * SSD_STATE]
        cm = xbc[:, SSD_WIDTH + SSD_GROUPS * SSD_STATE:]

        dt_t = dt_s[:, rows]
        cs_t = dt_t * a_neg
        lane16 = lax.broadcasted_iota(jnp.int32, (SSD_HEADS, CHUNK), 1)
        k = 1
        while k < CHUNK:
            cs_t = cs_t + jnp.where(lane16 >= k, pltpu.roll(cs_t, k, axis=1), 0.0)
            k *= 2

        pieces = list(_split3(cs_t)) + list(_split3(dt_t))
        pad = jnp.zeros((CHUNK - len(pieces) * SSD_HEADS, CHUNK), F32)
        packed = jnp.concatenate(pieces + [pad], axis=0).T
        bc = _dot(packed.astype(BF16), ee_ref[...])
        xdt = xs * bc[:, SSD_HEADS * LANES:]
        cb = []
        for g in range(SSD_GROUPS):
            gs = slice(g * SSD_STATE, (g + 1) * SSD_STATE)
            cb.append(lax.dot_general(cm[:, gs].astype(BF16), bm[:, gs].astype(BF16),
                                      (((1,), (1,)), ((), ())), preferred_element_type=F32))
        yield SEG_COST["decay"]

        y_off = []
        for g in range(SSD_GROUPS):
            gc = slice(g * SSD_GROUP_WIDTH, (g + 1) * SSD_GROUP_WIDTH)
            y_off.append(_dot(cm[:, g * SSD_STATE:(g + 1) * SSD_STATE].astype(BF16),
                              keep(state_s[:, gc]).astype(BF16)))
        yield SEG_COST["yoff"]

        y_diag, csg = [], []
        for j in range(SSD_HEADS // 2):
            g = j // (SSD_HEADS // 2 // SSD_GROUPS)
            lhs = []
            for r in (2 * j, 2 * j + 1):
                seg = bc[:, r * LANES:(r + 1) * LANES] - cs_t[r:r + 1, :]
                lhs.append((cb[g] * jnp.exp2(jnp.where(tril, seg, -jnp.inf))).astype(BF16))
            xp = xdt[:, j * LANES:(j + 1) * LANES]
            rhs = jnp.concatenate([jnp.where(lo_half, xp, 0.0), jnp.where(lo_half, 0.0, xp)], axis=0).astype(BF16)
            y_diag.append(_dot(jnp.concatenate(lhs, axis=1), rhs))
            csg.append(jnp.where(lo_half, bc[:, 2 * j * LANES:(2 * j + 1) * LANES],
                                 bc[:, (2 * j + 1) * LANES:(2 * j + 2) * LANES]))
            yield SEG_COST["head_pair"]
        csg = jnp.concatenate(csg, axis=1)
        cs_last = csg[CHUNK - 1:CHUNK, :]

        xd = (xdt * jnp.exp2(cs_last - csg)).astype(BF16)
        chunk_decay = jnp.exp2(cs_last)
        for g in range(SSD_GROUPS):
            gc = slice(g * SSD_GROUP_WIDTH, (g + 1) * SSD_GROUP_WIDTH)
            b_t = bm[:, g * SSD_STATE:(g + 1) * SSD_STATE].T.astype(BF16)
            state_s[:, gc] = keep(state_s[:, gc]) * chunk_decay[:, gc] + _dot(b_t, xd[:, gc])
        yield SEG_COST["state"]

        for g in range(SSD_GROUPS):
            gc = slice(g * SSD_GROUP_WIDTH, (g + 1) * SSD_GROUP_WIDTH)
            pairs = SSD_HEADS // 2 // SSD_GROUPS
            yg = (jnp.concatenate(y_diag[g * pairs:(g + 1) * pairs], axis=1) + jnp.exp2(csg[:, gc]) * y_off[g]
                  + dskip_ref[:, gc] * xs[:, gc])
            yg = yg * _silu(z_s[rows, gc])
            y_s[rows, GM_WIDTH + g * SSD_GROUP_WIDTH:GM_WIDTH + (g + 1) * SSD_GROUP_WIDTH] = _rms(yg).astype(BF16)
            yield SEG_COST["gate_half"]

    hist_s[...] = xbc_s[ts:ts + SUBLANES, :]


def _mixer_kernel(xa_ref, xc_ref, wuv_ref, wz_ref, wxbc_ref, wdt_ref, gv_ref, ws_ref, bsb_ref,
                  convw_ref, convb_ref, dtb_ref, alog_ref, dskip_ref, ee_ref, wout_ref,
                  o_ref,
                  uv0, z0, xbc0, dt0, uv1, z1, xbc1, dt1, y0, y1, wm_s, state_s, hist_s, xn_s, *,
                  tiles_per_seq):
    j = pl.program_id(0)
    ts = SEQ_TILE
    p0, p1 = (uv0, z0, xbc0, dt0), (uv1, z1, xbc1, dt1)

    @pl.when(j == 0)
    def _():
        for ref in (uv1, z1, xbc1, dt1, y0, y1, state_s, hist_s):
            ref[...] = jnp.zeros_like(ref)
        row = lax.broadcasted_iota(jnp.int32, (CHUNK, CHUNK), 0)
        lane = lax.broadcasted_iota(jnp.int32, (CHUNK, CHUNK), 1)
        for h in range(GM_HEADS):
            wm_s[h] = jnp.where(row >= lane, ws_ref[h], 0.0).astype(BF16)

    w_proj = (wuv_ref, wz_ref, wxbc_ref, wdt_ref, dtb_ref, wout_ref)
    w_proc = (gv_ref, bsb_ref, convw_ref, convb_ref, alog_ref, dskip_ref, ee_ref)
    first_even = (2 * j) % tiles_per_seq == 0

    main_total = _process_cost(ts // CHUNK)
    lo, hi = slice(0, ts), slice(ts, 2 * ts)
    _run_interleaved(_process_stage(p1, w_proc, wm_s, y1, state_s, hist_s, None), main_total,
                     _matmul_items(xa_ref.at[lo], xc_ref.at[lo], o_ref.at[lo], y0, w_proj, p0, xn_s))
    _run_interleaved(_process_stage(p0, w_proc, wm_s, y0, state_s, hist_s, first_even), main_total,
                     _matmul_items(xa_ref.at[hi], xc_ref.at[hi], o_ref.at[hi], y1, w_proj, p1, xn_s))


def _ffn_kernel(h_ref, p_ref, gmlp_ref, w1_ref, w2_ref, gple_ref, wgate_ref, wple_ref, gfin_ref, o_ref):
    h = h_ref[...]
    n = _rms(h, gmlp_ref[...]).astype(BF16)
    for j in range(D_FF // FF_BLOCK):
        blk = slice(j * FF_BLOCK, (j + 1) * FF_BLOCK)
        hid = jnp.maximum(_dot(n, w1_ref[:, blk]), 0.0)
        h = h + _dot((hid * hid).astype(BF16), w2_ref[blk, :])
    gate = jax.nn.sigmoid(_dot(_rms(h, gple_ref[...]).astype(BF16), wgate_ref[...]))
    h = h + gate * _dot(p_ref[...].astype(BF16), wple_ref[...])
    o_ref[...] = _rms(h, gfin_ref[...])


def _const_spec(shape):
    nd = len(shape)
    return pl.BlockSpec(shape, lambda *_: (0,) * nd, pipeline_mode=pl.Buffered(1))


def kernel(x, p, norm_mix_g, w_in, gm_v_norm_g, gm_ws, gm_bs, gm_out_norm_g, ssd_conv_w, ssd_conv_b, ssd_dt_bias,
           ssd_a_log, ssd_d, ssd_norm_g, w_out, norm_mlp_g, w_ff1, w_ff2, ple_norm_g, w_ple_gate, w_ple_proj,
           final_norm_g):
    b, s, d = x.shape
    t = b * s
    depth = w_in.shape[0]
    tiles_per_seq = s // SEQ_TILE
    assert depth == 1 and d == D_MODEL and s % (2 * SEQ_TILE) == 0 and t % FFN_TILE == 0
    ee = jnp.asarray(_expander_matrix(), BF16)
    row2 = lambda v: v.reshape(1, -1).astype(F32)
    i = 0

    w = (norm_mix_g[i].astype(F32)[:, None] * w_in[i].astype(F32)).astype(BF16)
    out_gain = jnp.concatenate([gm_out_norm_g[i], ssd_norm_g[i]]).astype(F32)
    w_o = (out_gain[:, None] * w_out[i].astype(F32)).astype(BF16)
    c0, c1, c2 = 2 * GM_WIDTH, 2 * GM_WIDTH + SSD_WIDTH, 2 * GM_WIDTH + SSD_WIDTH + SSD_CONV_CH
    w_dt = w[:, c2:].T
    dt_bias = jnp.broadcast_to(ssd_dt_bias[i].astype(F32)[:, None], (SSD_HEADS, SEQ_TILE))
    a_log = jnp.broadcast_to(ssd_a_log[i].astype(F32)[:, None], (SSD_HEADS, LANES))
    bsb = jnp.broadcast_to(gm_bs[i].astype(F32)[:, :, None], (GM_HEADS, CHUNK, LANES))
    d_skip = jnp.repeat(ssd_d[i].astype(F32), SSD_HEAD_DIM).reshape(1, SSD_WIDTH)
    mixer_in = [
        w[:, :c0], w[:, c0:c1], w[:, c1:c2], w_dt, row2(gm_v_norm_g[i]), gm_ws[i].astype(F32), bsb,
        ssd_conv_w[i].astype(F32), row2(ssd_conv_b[i]), dt_bias, a_log, d_skip, ee, w_o,
    ]
    nb = t // (2 * SEQ_TILE)
    x2 = x.reshape(t, D_MODEL)
    pbufs = [
        pltpu.VMEM((SEQ_TILE, 2 * GM_WIDTH), F32),
        pltpu.VMEM((SEQ_TILE, SSD_WIDTH), F32),
        pltpu.VMEM((SEQ_TILE + SUBLANES, SSD_CONV_CH), F32),
        pltpu.VMEM((SSD_HEADS, SEQ_TILE), F32),
    ]
    h = pl.pallas_call(
        functools.partial(_mixer_kernel, tiles_per_seq=tiles_per_seq),
        name="mixer",
        grid=(nb + 1,),
        in_specs=[pl.BlockSpec((2 * SEQ_TILE, D_MODEL), lambda j: (jnp.minimum(j, nb - 1), 0)),
                  pl.BlockSpec((2 * SEQ_TILE, D_MODEL), lambda j: (jnp.maximum(j - 1, 0), 0))]
                 + [_const_spec(a.shape) for a in mixer_in],
        out_specs=pl.BlockSpec((2 * SEQ_TILE, D_MODEL), lambda j: (jnp.maximum(j - 1, 0), 0)),
        out_shape=jax.ShapeDtypeStruct((t, D_MODEL), F32),
        scratch_shapes=pbufs + pbufs + [
            pltpu.VMEM((SEQ_TILE, GM_WIDTH + SSD_WIDTH), BF16),
            pltpu.VMEM((SEQ_TILE, GM_WIDTH + SSD_WIDTH), BF16),
            pltpu.VMEM((GM_HEADS, CHUNK, CHUNK), BF16),
            pltpu.VMEM((SSD_STATE, SSD_WIDTH), F32),
            pltpu.VMEM((SUBLANES, SSD_CONV_CH), F32),
            pltpu.VMEM((SEQ_TILE, D_MODEL), BF16),
        ],
        compiler_params=pltpu.CompilerParams(dimension_semantics=("arbitrary",), vmem_limit_bytes=VMEM_LIMIT),
    )(x2, x2, *mixer_in)

    ffn_in = [row2(norm_mlp_g[i]), w_ff1[i].astype(BF16), w_ff2[i].astype(BF16), row2(ple_norm_g[i]),
              w_ple_gate[i].astype(BF16), w_ple_proj[i].astype(BF16), row2(final_norm_g)]
    out = pl.pallas_call(
        _ffn_kernel,
        name="ffn",
        grid=(t // FFN_TILE,),
        in_specs=[pl.BlockSpec((FFN_TILE, D_MODEL), lambda ti: (ti, 0)),
                  pl.BlockSpec((FFN_TILE, D_PLE), lambda ti: (ti, 0))]
                 + [_const_spec(a.shape) for a in ffn_in],
        out_specs=pl.BlockSpec((FFN_TILE, D_MODEL), lambda ti: (ti, 0)),
        out_shape=jax.ShapeDtypeStruct((t, D_MODEL), F32),
        compiler_params=pltpu.CompilerParams(dimension_semantics=("arbitrary",), vmem_limit_bytes=VMEM_LIMIT),
    )(h, p[i].reshape(t, D_PLE), *ffn_in)
    return out.reshape(b, s, D_MODEL)
```

```python
import functools

import numpy as np
import jax
import jax.numpy as jnp
from jax import lax
from jax.experimental import pallas as pl
from jax.experimental.pallas import tpu as pltpu

D_MODEL = 1024
D_PLE = 256
D_FF = 4096
GM_WIDTH = 1024
GM_HEADS = 8
SSD_WIDTH = 1024
SSD_HEADS = 16
SSD_HEAD_DIM = 64
SSD_GROUPS = 2
SSD_GROUP_WIDTH = SSD_WIDTH // SSD_GROUPS
SSD_STATE = 128
SSD_CONV = 4
SSD_CONV_CH = SSD_WIDTH + 2 * SSD_GROUPS * SSD_STATE
CHUNK = 128
EPS = 1e-6
LOG2E = float(np.log2(np.e))

LANES = 128
SUBLANES = 8
SEQ_TILE = 256
FFN_TILE = 1024
FF_BLOCK = 1024
MM_SLAB = 512
OUT_SLAB = 512
CONV_SLAB = 512
LEAD = 0.0
VMEM_LIMIT = 56 * 1024 * 1024

F32 = jnp.float32
BF16 = jnp.bfloat16


def _dot(a, b):
    return jnp.dot(a, b, preferred_element_type=F32)


def _rms(x, g=None):
    ms = jnp.mean(x * x, axis=-1, keepdims=True)
    y = x * lax.rsqrt(ms + EPS)
    return y if g is None else y * g


def _silu(x):
    return x * jax.nn.sigmoid(x)


SEG_COST = dict(gmlp_pair=(130, 64), gmlp_norm=(190, 0), conv_slab=(230, 0), decay=(60, 64), yoff=(20, 128),
                head_pair=(60, 32), state=(150, 128), gate_half=(175, 0))
ITEM_COST = dict(out_slab=(160, 1024), norm=(280, 0), uv_slab=(250, 512), plain_slab=(64, 512), dt=(30, 300))


def _matmul_items(x_ref, xres_ref, o_ref, y_s, w, pbuf, xn_s):
    wuv_ref, wz_ref, wxbc_ref, wdt_ref, dtb_ref, wout_ref = w
    uv_s, z_s, xbc_s, dt_s = pbuf
    ts = x_ref.shape[0]

    def norm():
        xn_s[...] = _rms(x_ref[...]).astype(BF16)

    def uv_slab(cols):
        def run():
            uv = _dot(xn_s[...], wuv_ref[:, cols])
            uv_s[:, cols] = 0.5 * uv * (1.0 + lax.erf(uv * np.float32(np.sqrt(0.5))))
        return run

    def plain_slab(dst, w_ref, cols, row0):
        def run():
            dst[row0:row0 + ts, cols] = _dot(xn_s[...], w_ref[:, cols])
        return run

    def dt_item():
        dt_raw = lax.dot_general(wdt_ref[...], xn_s[...], (((1,), (1,)), ((), ())),
                                 preferred_element_type=F32) + dtb_ref[...]
        dt_s[...] = jnp.maximum(dt_raw, 0.0) + jnp.log1p(jnp.exp(-jnp.abs(dt_raw)))

    def out_slab(cols):
        def run():
            o_ref[:, cols] = xres_ref[:, cols] + _dot(y_s[...], wout_ref[:, cols])
        return run

    slabs = lambda width: [slice(c, c + MM_SLAB) for c in range(0, width, MM_SLAB)]
    items = [(ITEM_COST["norm"], norm)]
    items += [(ITEM_COST["uv_slab"], uv_slab(c)) for c in slabs(2 * GM_WIDTH)]
    items += [(ITEM_COST["plain_slab"], plain_slab(xbc_s, wxbc_ref, c, SUBLANES)) for c in slabs(SSD_CONV_CH)]
    items += [(ITEM_COST["plain_slab"], plain_slab(z_s, wz_ref, c, 0)) for c in slabs(SSD_WIDTH)]
    items.append((ITEM_COST["dt"], dt_item))
    items += [(ITEM_COST["out_slab"], out_slab(slice(c, c + OUT_SLAB))) for c in range(0, D_MODEL, OUT_SLAB)]
    return items


def _run_interleaved(main, main_total, items):
    side_total = sum(c[1] for c, _ in items)
    v_done = side_done = 0.0
    pending = list(items)
    for v, _ in main:
        v_done += v
        while pending and ((side_done + pending[0][0][1] / 2) * main_total[0]
                           <= (v_done + LEAD * main_total[0]) * side_total):
            (_, m_item), fn = pending.pop(0)
            fn()
            side_done += m_item
    for _, fn in pending:
        fn()


def _process_cost(n_chunks):
    per_chunk = ((SSD_CONV_CH // CONV_SLAB) * np.array(SEG_COST["conv_slab"]) + np.array(SEG_COST["decay"])
                 + np.array(SEG_COST["yoff"]) + (SSD_HEADS // 2) * np.array(SEG_COST["head_pair"])
                 + np.array(SEG_COST["state"]) + SSD_GROUPS * np.array(SEG_COST["gate_half"])
                 + np.array(SEG_COST["gmlp_norm"]))
    return tuple((GM_HEADS // 2) * np.array(SEG_COST["gmlp_pair"]) + n_chunks * per_chunk)


def _process_stage(pbuf, w, wm_s, y_s, state_s, hist_s, first):
    gv_ref, bsb_ref, convw_ref, convb_ref, alog_ref, dskip_ref = w
    uv_s, z_s, xbc_s, dt_s = pbuf
    ts = uv_s.shape[0]
    n_chunks = ts // CHUNK
    chunk_rows = [slice(c * CHUNK, (c + 1) * CHUNK) for c in range(n_chunks)]
    carried = (lambda v: v) if first is None else (lambda v: jnp.where(first, 0.0, v))

    row = lax.broadcasted_iota(jnp.int32, (CHUNK, CHUNK), 0)
    lane = lax.broadcasted_iota(jnp.int32, (CHUNK, CHUNK), 1)
    tril = row >= lane
    lo_half = lane < SSD_HEAD_DIM
    a_neg = -jnp.exp(alog_ref[...]) * LOG2E

    xbc_s[0:SUBLANES, :] = carried(hist_s[...])

    for h in range(GM_HEADS):
        cols = slice(h * LANES, (h + 1) * LANES)
        vcols = slice(GM_WIDTH + h * LANES, GM_WIDTH + (h + 1) * LANES)
        vn = jnp.concatenate([_rms(uv_s[rows, vcols], gv_ref[:, cols]).astype(BF16) for rows in chunk_rows], axis=1)
        mixed = _dot(wm_s[h], vn)
        for c, rows in enumerate(chunk_rows):
            uv_s[rows, cols] = uv_s[rows, cols] * (mixed[:, c * LANES:(c + 1) * LANES] + bsb_ref[h])
        if h % 2 == 1:
            yield SEG_COST["gmlp_pair"]

    for c, rows in enumerate(chunk_rows):
        r0 = c * CHUNK
        keep = carried if c == 0 else (lambda v: v)

        y_s[rows, 0:GM_WIDTH] = _rms(uv_s[rows, 0:GM_WIDTH]).astype(BF16)
        yield SEG_COST["gmlp_norm"]

        xbc = []
        for c0 in range(0, SSD_CONV_CH, CONV_SLAB):
            ccols = slice(c0, c0 + CONV_SLAB)
            blk = xbc_s[r0:r0 + SUBLANES + CHUNK, ccols]
            acc = convb_ref[:, ccols] + convw_ref[SSD_CONV - 1:SSD_CONV, ccols] * blk[SUBLANES:, :]
            for lag in range(1, SSD_CONV):
                k = SSD_CONV - 1 - lag
                acc = acc + convw_ref[k:k + 1, ccols] * pltpu.roll(blk, lag, axis=0)[SUBLANES:, :]
            xbc.append(_silu(acc))
            yield SEG_COST["conv_slab"]
        xbc = jnp.concatenate(xbc, axis=1)
        xs = xbc[:, :SSD_WIDTH]
        bm = xbc[:, SSD_WIDTH:SSD_WIDTH + SSD_GROUPS * SSD_STATE]
        cm = xbc[:, SSD_WIDTH + SSD_GROUPS * SSD_STATE:]

        dt_t = dt_s[:, rows]
        cs_t = dt_t * a_neg
        lane16 = lax.broadcasted_iota(jnp.int32, (SSD_HEADS, CHUNK), 1)
        k = 1
        while k < CHUNK:
            cs_t = cs_t + jnp.where(lane16 >= k, pltpu.roll(cs_t, k, axis=1), 0.0)
            k *= 2

        pad = jnp.zeros((CHUNK - 2 * SSD_HEADS, CHUNK), F32)
        cols = jnp.concatenate([cs_t, dt_t, pad], axis=0).T
        cb = []
        for g in range(SSD_GROUPS):
            gs = slice(g * SSD_STATE, (g + 1) * SSD_STATE)
            cb.append(lax.dot_general(cm[:, gs].astype(BF16), bm[:, gs].astype(BF16),
                                      (((1,), (1,)), ((), ())), preferred_element_type=F32))
        yield SEG_COST["decay"]

        y_off = []
        for g in range(SSD_GROUPS):
            gc = slice(g * SSD_GROUP_WIDTH, (g + 1) * SSD_GROUP_WIDTH)
            y_off.append(_dot(cm[:, g * SSD_STATE:(g + 1) * SSD_STATE].astype(BF16),
                              keep(state_s[:, gc]).astype(BF16)))
        yield SEG_COST["yoff"]

        y_diag, csg, xdt = [], [], []
        for j in range(SSD_HEADS // 2):
            g = j // (SSD_HEADS // 2 // SSD_GROUPS)
            r0, r1 = 2 * j, 2 * j + 1
            lhs = []
            for r in (r0, r1):
                seg = cols[:, r:r + 1] - cs_t[r:r + 1, :]
                lhs.append((cb[g] * jnp.exp2(jnp.where(tril, seg, -jnp.inf))).astype(BF16))
            dt_pair = jnp.where(lo_half, cols[:, SSD_HEADS + r0:SSD_HEADS + r0 + 1],
                                cols[:, SSD_HEADS + r1:SSD_HEADS + r1 + 1])
            xp = xs[:, j * LANES:(j + 1) * LANES] * dt_pair
            rhs = jnp.concatenate([jnp.where(lo_half, xp, 0.0), jnp.where(lo_half, 0.0, xp)], axis=0).astype(BF16)
            y_diag.append(_dot(jnp.concatenate(lhs, axis=1), rhs))
            csg.append(jnp.where(lo_half, cols[:, r0:r0 + 1], cols[:, r1:r1 + 1]))
            xdt.append(xp)
            yield SEG_COST["head_pair"]
        csg = jnp.concatenate(csg, axis=1)
        xdt = jnp.concatenate(xdt, axis=1)
        cs_last = csg[CHUNK - 1:CHUNK, :]

        xd = (xdt * jnp.exp2(cs_last - csg)).astype(BF16)
        chunk_decay = jnp.exp2(cs_last)
        for g in range(SSD_GROUPS):
            gc = slice(g * SSD_GROUP_WIDTH, (g + 1) * SSD_GROUP_WIDTH)
            b_t = bm[:, g * SSD_STATE:(g + 1) * SSD_STATE].T.astype(BF16)
            state_s[:, gc] = keep(state_s[:, gc]) * chunk_decay[:, gc] + _dot(b_t, xd[:, gc])
        yield SEG_COST["state"]

        for g in range(SSD_GROUPS):
            gc = slice(g * SSD_GROUP_WIDTH, (g + 1) * SSD_GROUP_WIDTH)
            pairs = SSD_HEADS // 2 // SSD_GROUPS
            yg = (jnp.concatenate(y_diag[g * pairs:(g + 1) * pairs], axis=1) + jnp.exp2(csg[:, gc]) * y_off[g]
                  + dskip_ref[:, gc] * xs[:, gc])
            yg = yg * _silu(z_s[rows, gc])
            y_s[rows, GM_WIDTH + g * SSD_GROUP_WIDTH:GM_WIDTH + (g + 1) * SSD_GROUP_WIDTH] = _rms(yg).astype(BF16)
            yield SEG_COST["gate_half"]

    hist_s[...] = xbc_s[ts:ts + SUBLANES, :]


def _mixer_kernel(xa_ref, xc_ref, wuv_ref, wz_ref, wxbc_ref, wdt_ref, gv_ref, ws_ref, bsb_ref,
                  convw_ref, convb_ref, dtb_ref, alog_ref, dskip_ref, wout_ref,
                  o_ref,
                  uv0, z0, xbc0, dt0, uv1, z1, xbc1, dt1, y0, y1, wm_s, state_s, hist_s, xn_s, *,
                  tiles_per_seq):
    j = pl.program_id(0)
    ts = SEQ_TILE
    p0, p1 = (uv0, z0, xbc0, dt0), (uv1, z1, xbc1, dt1)

    @pl.when(j == 0)
    def _():
        for ref in (uv1, z1, xbc1, dt1, y0, y1, state_s, hist_s):
            ref[...] = jnp.zeros_like(ref)
        row = lax.broadcasted_iota(jnp.int32, (CHUNK, CHUNK), 0)
        lane = lax.broadcasted_iota(jnp.int32, (CHUNK, CHUNK), 1)
        for h in range(GM_HEADS):
            wm_s[h] = jnp.where(row >= lane, ws_ref[h], 0.0).astype(BF16)

    w_proj = (wuv_ref, wz_ref, wxbc_ref, wdt_ref, dtb_ref, wout_ref)
    w_proc = (gv_ref, bsb_ref, convw_ref, convb_ref, alog_ref, dskip_ref)
    first_even = (2 * j) % tiles_per_seq == 0

    main_total = _process_cost(ts // CHUNK)
    lo, hi = slice(0, ts), slice(ts, 2 * ts)
    _run_interleaved(_process_stage(p1, w_proc, wm_s, y1, state_s, hist_s, None), main_total,
                     _matmul_items(xa_ref.at[lo], xc_ref.at[lo], o_ref.at[lo], y0, w_proj, p0, xn_s))
    _run_interleaved(_process_stage(p0, w_proc, wm_s, y0, state_s, hist_s, first_even), main_total,
                     _matmul_items(xa_ref.at[hi], xc_ref.at[hi], o_ref.at[hi], y1, w_proj, p1, xn_s))


def _ffn_kernel(h_ref, p_ref, gmlp_ref, w1_ref, w2_ref, gple_ref, wgate_ref, wple_ref, gfin_ref, o_ref):
    h = h_ref[...]
    n = _rms(h, gmlp_ref[...]).astype(BF16)
    for j in range(D_FF // FF_BLOCK):
        blk = slice(j * FF_BLOCK, (j + 1) * FF_BLOCK)
        hid = jnp.maximum(_dot(n, w1_ref[:, blk]), 0.0)
        h = h + _dot((hid * hid).astype(BF16), w2_ref[blk, :])
    gate = jax.nn.sigmoid(_dot(_rms(h, gple_ref[...]).astype(BF16), wgate_ref[...]))
    h = h + gate * _dot(p_ref[...].astype(BF16), wple_ref[...])
    o_ref[...] = _rms(h, gfin_ref[...])


def _const_spec(shape):
    nd = len(shape)
    return pl.BlockSpec(shape, lambda *_: (0,) * nd, pipeline_mode=pl.Buffered(1))


def kernel(x, p, norm_mix_g, w_in, gm_v_norm_g, gm_ws, gm_bs, gm_out_norm_g, ssd_conv_w, ssd_conv_b, ssd_dt_bias,
           ssd_a_log, ssd_d, ssd_norm_g, w_out, norm_mlp_g, w_ff1, w_ff2, ple_norm_g, w_ple_gate, w_ple_proj,
           final_norm_g):
    b, s, d = x.shape
    t = b * s
    depth = w_in.shape[0]
    tiles_per_seq = s // SEQ_TILE
    assert depth == 1 and d == D_MODEL and s % (2 * SEQ_TILE) == 0 and t % FFN_TILE == 0
    row2 = lambda v: v.reshape(1, -1).astype(F32)
    i = 0

    w = (norm_mix_g[i].astype(F32)[:, None] * w_in[i].astype(F32)).astype(BF16)
    out_gain = jnp.concatenate([gm_out_norm_g[i], ssd_norm_g[i]]).astype(F32)
    w_o = (out_gain[:, None] * w_out[i].astype(F32)).astype(BF16)
    c0, c1, c2 = 2 * GM_WIDTH, 2 * GM_WIDTH + SSD_WIDTH, 2 * GM_WIDTH + SSD_WIDTH + SSD_CONV_CH
    w_dt = w[:, c2:].T
    dt_bias = jnp.broadcast_to(ssd_dt_bias[i].astype(F32)[:, None], (SSD_HEADS, SEQ_TILE))
    a_log = jnp.broadcast_to(ssd_a_log[i].astype(F32)[:, None], (SSD_HEADS, LANES))
    bsb = jnp.broadcast_to(gm_bs[i].astype(F32)[:, :, None], (GM_HEADS, CHUNK, LANES))
    d_skip = jnp.repeat(ssd_d[i].astype(F32), SSD_HEAD_DIM).reshape(1, SSD_WIDTH)
    mixer_in = [
        w[:, :c0], w[:, c0:c1], w[:, c1:c2], w_dt, row2(gm_v_norm_g[i]), gm_ws[i].astype(F32), bsb,
        ssd_conv_w[i].astype(F32), row2(ssd_conv_b[i]), dt_bias, a_log, d_skip, w_o,
    ]
    nb = t // (2 * SEQ_TILE)
    x2 = x.reshape(t, D_MODEL)
    pbufs = [
        pltpu.VMEM((SEQ_TILE, 2 * GM_WIDTH), F32),
        pltpu.VMEM((SEQ_TILE, SSD_WIDTH), F32),
        pltpu.VMEM((SEQ_TILE + SUBLANES, SSD_CONV_CH), F32),
        pltpu.VMEM((SSD_HEADS, SEQ_TILE), F32),
    ]
    h = pl.pallas_call(
        functools.partial(_mixer_kernel, tiles_per_seq=tiles_per_seq),
        name="mixer",
        grid=(nb + 1,),
        in_specs=[pl.BlockSpec((2 * SEQ_TILE, D_MODEL), lambda j: (jnp.minimum(j, nb - 1), 0)),
                  pl.BlockSpec((2 * SEQ_TILE, D_MODEL), lambda j: (jnp.maximum(j - 1, 0), 0))]
                 + [_const_spec(a.shape) for a in mixer_in],
        out_specs=pl.BlockSpec((2 * SEQ_TILE, D_MODEL), lambda j: (jnp.maximum(j - 1, 0), 0)),
        out_shape=jax.ShapeDtypeStruct((t, D_MODEL), F32),
        scratch_shapes=pbufs + pbufs + [
            pltpu.VMEM((SEQ_TILE, GM_WIDTH + SSD_WIDTH), BF16),
            pltpu.VMEM((SEQ_TILE, GM_WIDTH + SSD_WIDTH), BF16),
            pltpu.VMEM((GM_HEADS, CHUNK, CHUNK), BF16),
            pltpu.VMEM((SSD_STATE, SSD_WIDTH), F32),
            pltpu.VMEM((SUBLANES, SSD_CONV_CH), F32),
            pltpu.VMEM((SEQ_TILE, D_MODEL), BF16),
        ],
        compiler_params=pltpu.CompilerParams(dimension_semantics=("arbitrary",), vmem_limit_bytes=VMEM_LIMIT),
    )(x2, x2, *mixer_in)

    ffn_in = [row2(norm_mlp_g[i]), w_ff1[i].astype(BF16), w_ff2[i].astype(BF16), row2(ple_norm_g[i]),
              w_ple_gate[i].astype(BF16), w_ple_proj[i].astype(BF16), row2(final_norm_g)]
    out = pl.pallas_call(
        _ffn_kernel,
        name="ffn",
        grid=(t // FFN_TILE,),
        in_specs=[pl.BlockSpec((FFN_TILE, D_MODEL), lambda ti: (ti, 0)),
                  pl.BlockSpec((FFN_TILE, D_PLE), lambda ti: (ti, 0))]
                 + [_const_spec(a.shape) for a in ffn_in],
        out_specs=pl.BlockSpec((FFN_TILE, D_MODEL), lambda ti: (ti, 0)),
        out_shape=jax.ShapeDtypeStruct((t, D_MODEL), F32),
        compiler_params=pltpu.CompilerParams(dimension_semantics=("arbitrary",), vmem_limit_bytes=VMEM_LIMIT),
    )(h, p[i].reshape(t, D_PLE), *ffn_in)
    return out.reshape(b, s, D_MODEL)
```

```python
import functools

import numpy as np
import jax
import jax.numpy as jnp
from jax import lax
from jax.experimental import pallas as pl
from jax.experimental.pallas import tpu as pltpu

D_MODEL = 1024
D_PLE = 256
D_FF = 4096
GM_WIDTH = 1024
GM_HEADS = 8
SSD_WIDTH = 1024
SSD_HEADS = 16
SSD_HEAD_DIM = 64
SSD_GROUPS = 2
SSD_GROUP_WIDTH = SSD_WIDTH // SSD_GROUPS
SSD_STATE = 128
SSD_CONV = 4
SSD_CONV_CH = SSD_WIDTH + 2 * SSD_GROUPS * SSD_STATE
CHUNK = 128
EPS = 1e-6
LOG2E = float(np.log2(np.e))

LANES = 128
SUBLANES = 8
SEQ_TILE = 256
FFN_TILE = 1024
FF_BLOCK = 1024
MM_SLAB = 512
OUT_SLAB = 512
CONV_SLAB = 512
LEAD = 0.0
VMEM_LIMIT = 56 * 1024 * 1024

F32 = jnp.float32
BF16 = jnp.bfloat16


def _dot(a, b):
    return jnp.dot(a, b, preferred_element_type=F32)


def _rms(x, g=None):
    ms = jnp.mean(x * x, axis=-1, keepdims=True)
    y = x * lax.rsqrt(ms + EPS)
    return y if g is None else y * g


def _silu(x):
    h = 0.5 * x
    return h + h * jnp.tanh(h)


SEG_COST = dict(gmlp_pair=(130, 64), gmlp_norm=(190, 0), conv_slab=(230, 0), decay=(60, 64), yoff=(20, 128),
                head_pair=(60, 32), state=(150, 128), gate_half=(175, 0))
ITEM_COST = dict(out_slab=(160, 1024), norm=(280, 0), uv_slab=(250, 512), plain_slab=(64, 512), dt=(30, 300))


def _matmul_items(x_ref, xres_ref, o_ref, y_s, w, pbuf, xn_s):
    wuv_ref, wz_ref, wxbc_ref, wdt_ref, dtb_ref, wout_ref = w
    uv_s, z_s, xbc_s, dt_s = pbuf
    ts = x_ref.shape[0]

    def norm():
        xn_s[...] = _rms(x_ref[...]).astype(BF16)

    def uv_slab(cols):
        def run():
            uv = _dot(xn_s[...], wuv_ref[:, cols])
            uv_s[:, cols] = 0.5 * uv * (1.0 + lax.erf(uv * np.float32(np.sqrt(0.5))))
        return run

    def plain_slab(dst, w_ref, cols, row0):
        def run():
            dst[row0:row0 + ts, cols] = _dot(xn_s[...], w_ref[:, cols])
        return run

    def dt_item():
        dt_raw = lax.dot_general(wdt_ref[...], xn_s[...], (((1,), (1,)), ((), ())),
                                 preferred_element_type=F32) + dtb_ref[...]
        dt_s[...] = jnp.maximum(dt_raw, 0.0) + jnp.log1p(jnp.exp(-jnp.abs(dt_raw)))

    def out_slab(cols):
        def run():
            o_ref[:, cols] = xres_ref[:, cols] + _dot(y_s[...], wout_ref[:, cols])
        return run

    slabs = lambda width: [slice(c, c + MM_SLAB) for c in range(0, width, MM_SLAB)]
    items = [(ITEM_COST["out_slab"], out_slab(slice(c, c + OUT_SLAB))) for c in range(0, D_MODEL, OUT_SLAB)]
    items.append((ITEM_COST["norm"], norm))
    items += [(ITEM_COST["uv_slab"], uv_slab(c)) for c in slabs(2 * GM_WIDTH)]
    items += [(ITEM_COST["plain_slab"], plain_slab(xbc_s, wxbc_ref, c, SUBLANES)) for c in slabs(SSD_CONV_CH)]
    items += [(ITEM_COST["plain_slab"], plain_slab(z_s, wz_ref, c, 0)) for c in slabs(SSD_WIDTH)]
    items.append((ITEM_COST["dt"], dt_item))
    return items


def _run_interleaved(main, main_total, items):
    side_total = sum(c[1] for c, _ in items)
    v_done = side_done = 0.0
    pending = list(items)
    for v, _ in main:
        v_done += v
        while pending and ((side_done + pending[0][0][1] / 2) * main_total[0]
                           <= (v_done + LEAD * main_total[0]) * side_total):
            (_, m_item), fn = pending.pop(0)
            fn()
            side_done += m_item
    for _, fn in pending:
        fn()


def _process_cost(n_chunks):
    per_chunk = ((SSD_CONV_CH // CONV_SLAB) * np.array(SEG_COST["conv_slab"]) + np.array(SEG_COST["decay"])
                 + np.array(SEG_COST["yoff"]) + (SSD_HEADS // 2) * np.array(SEG_COST["head_pair"])
                 + np.array(SEG_COST["state"]) + SSD_GROUPS * np.array(SEG_COST["gate_half"])
                 + np.array(SEG_COST["gmlp_norm"]))
    return tuple((GM_HEADS // 2) * np.array(SEG_COST["gmlp_pair"]) + n_chunks * per_chunk)


def _process_stage(pbuf, w, wm_s, y_s, state_s, hist_s, first):
    gv_ref, bsb_ref, convw_ref, convb_ref, alog_ref, dskip_ref = w
    uv_s, z_s, xbc_s, dt_s = pbuf
    ts = uv_s.shape[0]
    n_chunks = ts // CHUNK
    chunk_rows = [slice(c * CHUNK, (c + 1) * CHUNK) for c in range(n_chunks)]
    carried = (lambda v: v) if first is None else (lambda v: jnp.where(first, 0.0, v))

    row = lax.broadcasted_iota(jnp.int32, (CHUNK, CHUNK), 0)
    lane = lax.broadcasted_iota(jnp.int32, (CHUNK, CHUNK), 1)
    tril = row >= lane
    lo_half = lane < SSD_HEAD_DIM
    a_neg = -jnp.exp(alog_ref[...]) * LOG2E

    xbc_s[0:SUBLANES, :] = carried(hist_s[...])

    for h in range(GM_HEADS):
        cols = slice(h * LANES, (h + 1) * LANES)
        vcols = slice(GM_WIDTH + h * LANES, GM_WIDTH + (h + 1) * LANES)
        vn = jnp.concatenate([_rms(uv_s[rows, vcols], gv_ref[:, cols]).astype(BF16) for rows in chunk_rows], axis=1)
        mixed = _dot(wm_s[h], vn)
        for c, rows in enumerate(chunk_rows):
            uv_s[rows, cols] = uv_s[rows, cols] * (mixed[:, c * LANES:(c + 1) * LANES] + bsb_ref[h])
        if h % 2 == 1:
            yield SEG_COST["gmlp_pair"]

    for c, rows in enumerate(chunk_rows):
        r0 = c * CHUNK
        keep = carried if c == 0 else (lambda v: v)

        y_s[rows, 0:GM_WIDTH] = _rms(uv_s[rows, 0:GM_WIDTH]).astype(BF16)
        yield SEG_COST["gmlp_norm"]

        xbc = []
        for c0 in range(0, SSD_CONV_CH, CONV_SLAB):
            ccols = slice(c0, c0 + CONV_SLAB)
            blk = xbc_s[r0:r0 + SUBLANES + CHUNK, ccols]
            acc = convb_ref[:, ccols] + convw_ref[SSD_CONV - 1:SSD_CONV, ccols] * blk[SUBLANES:, :]
            for lag in range(1, SSD_CONV):
                k = SSD_CONV - 1 - lag
                acc = acc + convw_ref[k:k + 1, ccols] * pltpu.roll(blk, lag, axis=0)[SUBLANES:, :]
            xbc.append(_silu(acc))
            yield SEG_COST["conv_slab"]
        xbc = jnp.concatenate(xbc, axis=1)
        xs = xbc[:, :SSD_WIDTH]
        bm = xbc[:, SSD_WIDTH:SSD_WIDTH + SSD_GROUPS * SSD_STATE]
        cm = xbc[:, SSD_WIDTH + SSD_GROUPS * SSD_STATE:]

        dt_t = dt_s[:, rows]
        cs_t = dt_t * a_neg
        lane16 = lax.broadcasted_iota(jnp.int32, (SSD_HEADS, CHUNK), 1)
        k = 1
        while k < CHUNK:
            cs_t = cs_t + jnp.where(lane16 >= k, pltpu.roll(cs_t, k, axis=1), 0.0)
            k *= 2

        pad = jnp.zeros((CHUNK - 2 * SSD_HEADS, CHUNK), F32)
        cols = jnp.concatenate([cs_t, dt_t, pad], axis=0).T
        cb = []
        for g in range(SSD_GROUPS):
            gs = slice(g * SSD_STATE, (g + 1) * SSD_STATE)
            cb.append(lax.dot_general(cm[:, gs].astype(BF16), bm[:, gs].astype(BF16),
                                      (((1,), (1,)), ((), ())), preferred_element_type=F32))
        yield SEG_COST["decay"]

        y_off = []
        for g in range(SSD_GROUPS):
            gc = slice(g * SSD_GROUP_WIDTH, (g + 1) * SSD_GROUP_WIDTH)
            y_off.append(_dot(cm[:, g * SSD_STATE:(g + 1) * SSD_STATE].astype(BF16),
                              keep(state_s[:, gc]).astype(BF16)))
        yield SEG_COST["yoff"]

        y_diag, csg, xdt = [], [], []
        for j in range(SSD_HEADS // 2):
            g = j // (SSD_HEADS // 2 // SSD_GROUPS)
            r0, r1 = 2 * j, 2 * j + 1
            lhs = []
            for r in (r0, r1):
                seg = cols[:, r:r + 1] - cs_t[r:r + 1, :]
                lhs.append((cb[g] * jnp.exp2(jnp.where(tril, seg, -jnp.inf))).astype(BF16))
            dt_pair = jnp.where(lo_half, cols[:, SSD_HEADS + r0:SSD_HEADS + r0 + 1],
                                cols[:, SSD_HEADS + r1:SSD_HEADS + r1 + 1])
            xp = xs[:, j * LANES:(j + 1) * LANES] * dt_pair
            rhs = jnp.concatenate([jnp.where(lo_half, xp, 0.0), jnp.where(lo_half, 0.0, xp)], axis=0).astype(BF16)
            y_diag.append(_dot(jnp.concatenate(lhs, axis=1), rhs))
            csg.append(jnp.where(lo_half, cols[:, r0:r0 + 1], cols[:, r1:r1 + 1]))
            xdt.append(xp)
            yield SEG_COST["head_pair"]
        csg = jnp.concatenate(csg, axis=1)
        xdt = jnp.concatenate(xdt, axis=1)
        cs_last = csg[CHUNK - 1:CHUNK, :]

        xd = (xdt * jnp.exp2(cs_last - csg)).astype(BF16)
        chunk_decay = jnp.exp2(cs_last)
        for g in range(SSD_GROUPS):
            gc = slice(g * SSD_GROUP_WIDTH, (g + 1) * SSD_GROUP_WIDTH)
            b_t = bm[:, g * SSD_STATE:(g + 1) * SSD_STATE].T.astype(BF16)
            state_s[:, gc] = keep(state_s[:, gc]) * chunk_decay[:, gc] + _dot(b_t, xd[:, gc])
        yield SEG_COST["state"]

        for g in range(SSD_GROUPS):
            gc = slice(g * SSD_GROUP_WIDTH, (g + 1) * SSD_GROUP_WIDTH)
            pairs = SSD_HEADS // 2 // SSD_GROUPS
            yg = (jnp.concatenate(y_diag[g * pairs:(g + 1) * pairs], axis=1) + jnp.exp2(csg[:, gc]) * y_off[g]
                  + dskip_ref[:, gc] * xs[:, gc])
            yg = yg * _silu(z_s[rows, gc])
            y_s[rows, GM_WIDTH + g * SSD_GROUP_WIDTH:GM_WIDTH + (g + 1) * SSD_GROUP_WIDTH] = _rms(yg).astype(BF16)
            yield SEG_COST["gate_half"]

    hist_s[...] = xbc_s[ts:ts + SUBLANES, :]


def _mixer_kernel(xa_ref, xc_ref, wuv_ref, wz_ref, wxbc_ref, wdt_ref, gv_ref, ws_ref, bsb_ref,
                  convw_ref, convb_ref, dtb_ref, alog_ref, dskip_ref, wout_ref,
                  o_ref,
                  uv0, z0, xbc0, dt0, uv1, z1, xbc1, dt1, y0, y1, wm_s, state_s, hist_s, xn_s, *,
                  tiles_per_seq):
    j = pl.program_id(0)
    ts = SEQ_TILE
    p0, p1 = (uv0, z0, xbc0, dt0), (uv1, z1, xbc1, dt1)

    @pl.when(j == 0)
    def _():
        for ref in (uv1, z1, xbc1, dt1, y0, y1, state_s, hist_s):
            ref[...] = jnp.zeros_like(ref)
        row = lax.broadcasted_iota(jnp.int32, (CHUNK, CHUNK), 0)
        lane = lax.broadcasted_iota(jnp.int32, (CHUNK, CHUNK), 1)
        for h in range(GM_HEADS):
            wm_s[h] = jnp.where(row >= lane, ws_ref[h], 0.0).astype(BF16)

    w_proj = (wuv_ref, wz_ref, wxbc_ref, wdt_ref, dtb_ref, wout_ref)
    w_proc = (gv_ref, bsb_ref, convw_ref, convb_ref, alog_ref, dskip_ref)
    first_even = (2 * j) % tiles_per_seq == 0

    main_total = _process_cost(ts // CHUNK)
    lo, hi = slice(0, ts), slice(ts, 2 * ts)
    _run_interleaved(_process_stage(p1, w_proc, wm_s, y1, state_s, hist_s, None), main_total,
                     _matmul_items(xa_ref.at[lo], xc_ref.at[lo], o_ref.at[lo], y0, w_proj, p0, xn_s))
    _run_interleaved(_process_stage(p0, w_proc, wm_s, y0, state_s, hist_s, first_even), main_total,
                     _matmul_items(xa_ref.at[hi], xc_ref.at[hi], o_ref.at[hi], y1, w_proj, p1, xn_s))


def _ffn_kernel(h_ref, p_ref, gmlp_ref, w1_ref, w2_ref, gple_ref, wgate_ref, wple_ref, gfin_ref, o_ref):
    h = h_ref[...]
    n = _rms(h, gmlp_ref[...]).astype(BF16)
    for j in range(D_FF // FF_BLOCK):
        blk = slice(j * FF_BLOCK, (j + 1) * FF_BLOCK)
        hid = jnp.maximum(_dot(n, w1_ref[:, blk]), 0.0)
        h = h + _dot((hid * hid).astype(BF16), w2_ref[blk, :])
    gate = jax.nn.sigmoid(_dot(_rms(h, gple_ref[...]).astype(BF16), wgate_ref[...]))
    h = h + gate * _dot(p_ref[...].astype(BF16), wple_ref[...])
    o_ref[...] = _rms(h, gfin_ref[...])


def _const_spec(shape):
    nd = len(shape)
    return pl.BlockSpec(shape, lambda *_: (0,) * nd, pipeline_mode=pl.Buffered(1))


def kernel(x, p, norm_mix_g, w_in, gm_v_norm_g, gm_ws, gm_bs, gm_out_norm_g, ssd_conv_w, ssd_conv_b, ssd_dt_bias,
           ssd_a_log, ssd_d, ssd_norm_g, w_out, norm_mlp_g, w_ff1, w_ff2, ple_norm_g, w_ple_gate, w_ple_proj,
           final_norm_g):
    b, s, d = x.shape
    t = b * s
    depth = w_in.shape[0]
    tiles_per_seq = s // SEQ_TILE
    assert depth == 1 and d == D_MODEL and s % (2 * SEQ_TILE) == 0 and t % FFN_TILE == 0
    row2 = lambda v: v.reshape(1, -1).astype(F32)
    i = 0

    w = (norm_mix_g[i].astype(F32)[:, None] * w_in[i].astype(F32)).astype(BF16)
    out_gain = jnp.concatenate([gm_out_norm_g[i], ssd_norm_g[i]]).astype(F32)
    w_o = (out_gain[:, None] * w_out[i].astype(F32)).astype(BF16)
    c0, c1, c2 = 2 * GM_WIDTH, 2 * GM_WIDTH + SSD_WIDTH, 2 * GM_WIDTH + SSD_WIDTH + SSD_CONV_CH
    w_dt = w[:, c2:].T
    dt_bias = jnp.broadcast_to(ssd_dt_bias[i].astype(F32)[:, None], (SSD_HEADS, SEQ_TILE))
    a_log = jnp.broadcast_to(ssd_a_log[i].astype(F32)[:, None], (SSD_HEADS, LANES))
    bsb = jnp.broadcast_to(gm_bs[i].astype(F32)[:, :, None], (GM_HEADS, CHUNK, LANES))
    d_skip = jnp.repeat(ssd_d[i].astype(F32), SSD_HEAD_DIM).reshape(1, SSD_WIDTH)
    mixer_in = [
        w[:, :c0], w[:, c0:c1], w[:, c1:c2], w_dt, row2(gm_v_norm_g[i]), gm_ws[i].astype(F32), bsb,
        ssd_conv_w[i].astype(F32), row2(ssd_conv_b[i]), dt_bias, a_log, d_skip, w_o,
    ]
    nb = t // (2 * SEQ_TILE)
    x2 = x.reshape(t, D_MODEL)
    pbufs = [
        pltpu.VMEM((SEQ_TILE, 2 * GM_WIDTH), F32),
        pltpu.VMEM((SEQ_TILE, SSD_WIDTH), F32),
        pltpu.VMEM((SEQ_TILE + SUBLANES, SSD_CONV_CH), F32),
        pltpu.VMEM((SSD_HEADS, SEQ_TILE), F32),
    ]
    h = pl.pallas_call(
        functools.partial(_mixer_kernel, tiles_per_seq=tiles_per_seq),
        name="mixer",
        grid=(nb + 1,),
        in_specs=[pl.BlockSpec((2 * SEQ_TILE, D_MODEL), lambda j: (jnp.minimum(j, nb - 1), 0)),
                  pl.BlockSpec((2 * SEQ_TILE, D_MODEL), lambda j: (jnp.maximum(j - 1, 0), 0))]
                 + [_const_spec(a.shape) for a in mixer_in],
        out_specs=pl.BlockSpec((2 * SEQ_TILE, D_MODEL), lambda j: (jnp.maximum(j - 1, 0), 0)),
        out_shape=jax.ShapeDtypeStruct((t, D_MODEL), F32),
        scratch_shapes=pbufs + pbufs + [
            pltpu.VMEM((SEQ_TILE, GM_WIDTH + SSD_WIDTH), BF16),
            pltpu.VMEM((SEQ_TILE, GM_WIDTH + SSD_WIDTH), BF16),
            pltpu.VMEM((GM_HEADS, CHUNK, CHUNK), BF16),
            pltpu.VMEM((SSD_STATE, SSD_WIDTH), F32),
            pltpu.VMEM((SUBLANES, SSD_CONV_CH), F32),
            pltpu.VMEM((SEQ_TILE, D_MODEL), BF16),
        ],
        compiler_params=pltpu.CompilerParams(dimension_semantics=("arbitrary",), vmem_limit_bytes=VMEM_LIMIT),
    )(x2, x2, *mixer_in)

    ffn_in = [row2(norm_mlp_g[i]), w_ff1[i].astype(BF16), w_ff2[i].astype(BF16), row2(ple_norm_g[i]),
              w_ple_gate[i].astype(BF16), w_ple_proj[i].astype(BF16), row2(final_norm_g)]
    out = pl.pallas_call(
        _ffn_kernel,
        name="ffn",
        grid=(t // FFN_TILE,),
        in_specs=[pl.BlockSpec((FFN_TILE, D_MODEL), lambda ti: (ti, 0)),
                  pl.BlockSpec((FFN_TILE, D_PLE), lambda ti: (ti, 0))]
                 + [_const_spec(a.shape) for a in ffn_in],
        out_specs=pl.BlockSpec((FFN_TILE, D_MODEL), lambda ti: (ti, 0)),
        out_shape=jax.ShapeDtypeStruct((t, D_MODEL), F32),
        compiler_params=pltpu.CompilerParams(dimension_semantics=("arbitrary",), vmem_limit_bytes=VMEM_LIMIT),
    )(h, p[i].reshape(t, D_PLE), *ffn_in)
    return out.reshape(b, s, D_MODEL)
```

```python
import functools

import numpy as np
import jax
import jax.numpy as jnp
from jax import lax
from jax.experimental import pallas as pl
from jax.experimental.pallas import tpu as pltpu

D_MODEL = 1024
D_PLE = 256
D_FF = 4096
GM_WIDTH = 1024
GM_HEADS = 8
SSD_WIDTH = 1024
SSD_HEADS = 16
SSD_HEAD_DIM = 64
SSD_GROUPS = 2
SSD_GROUP_WIDTH = SSD_WIDTH // SSD_GROUPS
SSD_STATE = 128
SSD_CONV = 4
SSD_CONV_CH = SSD_WIDTH + 2 * SSD_GROUPS * SSD_STATE
CHUNK = 128
EPS = 1e-6
LOG2E = float(np.log2(np.e))

LANES = 128
SUBLANES = 8
SEQ_TILE = 256
FFN_TILE = 1024
FF_BLOCK = 1024
MM_SLAB = 512
OUT_SLAB = 512
CONV_SLAB = 512
LEAD = 0.0
VMEM_LIMIT = 56 * 1024 * 1024

F32 = jnp.float32
BF16 = jnp.bfloat16


def _dot(a, b):
    return jnp.dot(a, b, preferred_element_type=F32)


def _rms(x, g=None):
    ms = jnp.mean(x * x, axis=-1, keepdims=True)
    y = x * lax.rsqrt(ms + EPS)
    return y if g is None else y * g


def _silu(x):
    h = 0.5 * x
    return h + h * jnp.tanh(h)


SEG_COST = dict(gmlp_pair=(130, 64), gmlp_norm=(190, 0), conv_slab=(230, 0), decay=(60, 64), yoff=(20, 128),
                head_pair=(60, 32), state=(150, 128), gate_half=(175, 0))
ITEM_COST = dict(out_slab=(160, 1024), norm=(280, 0), uv_slab=(250, 512), plain_slab=(64, 512), dt=(30, 300))


def _matmul_items(x_ref, xres_ref, o_ref, y_s, w, pbuf, xn_s):
    wuv_ref, wz_ref, wxbc_ref, wdt_ref, dtb_ref, wout_ref = w
    uv_s, z_s, xbc_s, dt_s = pbuf
    ts = x_ref.shape[0]

    def norm():
        xn_s[...] = _rms(x_ref[...]).astype(BF16)

    def uv_slab(cols):
        def run():
            uv = _dot(xn_s[...], wuv_ref[:, cols])
            uv_s[:, cols] = 0.5 * uv * (1.0 + lax.erf(uv * np.float32(np.sqrt(0.5))))
        return run

    def plain_slab(dst, w_ref, cols, row0):
        def run():
            dst[row0:row0 + ts, cols] = _dot(xn_s[...], w_ref[:, cols])
        return run

    def dt_item():
        dt_raw = lax.dot_general(wdt_ref[...], xn_s[...], (((1,), (1,)), ((), ())),
                                 preferred_element_type=F32) + dtb_ref[...]
        dt_s[...] = jnp.maximum(dt_raw, 0.0) + jnp.log1p(jnp.exp(-jnp.abs(dt_raw)))

    def out_slab(cols):
        def run():
            o_ref[:, cols] = xres_ref[:, cols] + _dot(y_s[...], wout_ref[:, cols])
        return run

    slabs = lambda width: [slice(c, c + MM_SLAB) for c in range(0, width, MM_SLAB)]
    items = [(ITEM_COST["out_slab"], out_slab(slice(c, c + OUT_SLAB))) for c in range(0, D_MODEL, OUT_SLAB)]
    items.append((ITEM_COST["norm"], norm))
    items += [(ITEM_COST["uv_slab"], uv_slab(c)) for c in slabs(2 * GM_WIDTH)]
    items += [(ITEM_COST["plain_slab"], plain_slab(xbc_s, wxbc_ref, c, SUBLANES)) for c in slabs(SSD_CONV_CH)]
    items += [(ITEM_COST["plain_slab"], plain_slab(z_s, wz_ref, c, 0)) for c in slabs(SSD_WIDTH)]
    items.append((ITEM_COST["dt"], dt_item))
    return items


def _run_interleaved(main, main_total, items):
    side_total = sum(c[1] for c, _ in items)
    v_done = side_done = 0.0
    pending = list(items)
    for v, _ in main:
        v_done += v
        while pending and ((side_done + pending[0][0][1] / 2) * main_total[0]
                           <= (v_done + LEAD * main_total[0]) * side_total):
            (_, m_item), fn = pending.pop(0)
            fn()
            side_done += m_item
    for _, fn in pending:
        fn()


def _process_cost(n_chunks):
    per_chunk = ((SSD_CONV_CH // CONV_SLAB) * np.array(SEG_COST["conv_slab"]) + np.array(SEG_COST["decay"])
                 + np.array(SEG_COST["yoff"]) + (SSD_HEADS // 2) * np.array(SEG_COST["head_pair"])
                 + np.array(SEG_COST["state"]) + SSD_GROUPS * np.array(SEG_COST["gate_half"])
                 + np.array(SEG_COST["gmlp_norm"]))
    return tuple((GM_HEADS // 2) * np.array(SEG_COST["gmlp_pair"]) + n_chunks * per_chunk)


def _process_stage(pbuf, w, wm_s, y_s, state_s, hist_s, first):
    gv_ref, bsb_ref, convw_ref, convb_ref, alog_ref, dskip_ref = w
    uv_s, z_s, xbc_s, dt_s = pbuf
    ts = uv_s.shape[0]
    n_chunks = ts // CHUNK
    chunk_rows = [slice(c * CHUNK, (c + 1) * CHUNK) for c in range(n_chunks)]
    carried = (lambda v: v) if first is None else (lambda v: jnp.where(first, 0.0, v))

    row = lax.broadcasted_iota(jnp.int32, (CHUNK, CHUNK), 0)
    lane = lax.broadcasted_iota(jnp.int32, (CHUNK, CHUNK), 1)
    tril = row >= lane
    lo_half = lane < SSD_HEAD_DIM
    a_neg = -jnp.exp(alog_ref[...]) * LOG2E

    xbc_s[0:SUBLANES, :] = carried(hist_s[...])

    conv_out = []
    for c in range(n_chunks):
        r0 = c * CHUNK
        parts = []
        for c0 in range(0, SSD_CONV_CH, CONV_SLAB):
            ccols = slice(c0, c0 + CONV_SLAB)
            blk = xbc_s[r0:r0 + SUBLANES + CHUNK, ccols]
            acc = convb_ref[:, ccols] + convw_ref[SSD_CONV - 1:SSD_CONV, ccols] * blk[SUBLANES:, :]
            for lag in range(1, SSD_CONV):
                k = SSD_CONV - 1 - lag
                acc = acc + convw_ref[k:k + 1, ccols] * pltpu.roll(blk, lag, axis=0)[SUBLANES:, :]
            parts.append(_silu(acc))
            yield SEG_COST["conv_slab"]
        conv_out.append(jnp.concatenate(parts, axis=1))

    for c, rows in enumerate(chunk_rows):
        keep = carried if c == 0 else (lambda v: v)
        xbc = conv_out[c]
        xs = xbc[:, :SSD_WIDTH]
        bm = xbc[:, SSD_WIDTH:SSD_WIDTH + SSD_GROUPS * SSD_STATE]
        cm = xbc[:, SSD_WIDTH + SSD_GROUPS * SSD_STATE:]

        dt_t = dt_s[:, rows]
        cs_t = dt_t * a_neg
        lane16 = lax.broadcasted_iota(jnp.int32, (SSD_HEADS, CHUNK), 1)
        k = 1
        while k < CHUNK:
            cs_t = cs_t + jnp.where(lane16 >= k, pltpu.roll(cs_t, k, axis=1), 0.0)
            k *= 2

        pad = jnp.zeros((CHUNK - 2 * SSD_HEADS, CHUNK), F32)
        cols = jnp.concatenate([cs_t, dt_t, pad], axis=0).T
        cb = []
        for g in range(SSD_GROUPS):
            gs = slice(g * SSD_STATE, (g + 1) * SSD_STATE)
            cb.append(lax.dot_general(cm[:, gs].astype(BF16), bm[:, gs].astype(BF16),
                                      (((1,), (1,)), ((), ())), preferred_element_type=F32))
        yield SEG_COST["decay"]

        y_off = []
        for g in range(SSD_GROUPS):
            gc = slice(g * SSD_GROUP_WIDTH, (g + 1) * SSD_GROUP_WIDTH)
            y_off.append(_dot(cm[:, g * SSD_STATE:(g + 1) * SSD_STATE].astype(BF16),
                              keep(state_s[:, gc]).astype(BF16)))
        yield SEG_COST["yoff"]

        y_diag, csg, xdt = [], [], []
        for j in range(SSD_HEADS // 2):
            g = j // (SSD_HEADS // 2 // SSD_GROUPS)
            r0, r1 = 2 * j, 2 * j + 1
            lhs = []
            for r in (r0, r1):
                seg = cols[:, r:r + 1] - cs_t[r:r + 1, :]
                lhs.append((cb[g] * jnp.exp2(jnp.where(tril, seg, -jnp.inf))).astype(BF16))
            dt_pair = jnp.where(lo_half, cols[:, SSD_HEADS + r0:SSD_HEADS + r0 + 1],
                                cols[:, SSD_HEADS + r1:SSD_HEADS + r1 + 1])
            xp = xs[:, j * LANES:(j + 1) * LANES] * dt_pair
            rhs = jnp.concatenate([jnp.where(lo_half, xp, 0.0), jnp.where(lo_half, 0.0, xp)], axis=0).astype(BF16)
            y_diag.append(_dot(jnp.concatenate(lhs, axis=1), rhs))
            csg.append(jnp.where(lo_half, cols[:, r0:r0 + 1], cols[:, r1:r1 + 1]))
            xdt.append(xp)
            yield SEG_COST["head_pair"]
        csg = jnp.concatenate(csg, axis=1)
        xdt = jnp.concatenate(xdt, axis=1)
        cs_last = csg[CHUNK - 1:CHUNK, :]

        for g in range(SSD_GROUPS):
            gc = slice(g * SSD_GROUP_WIDTH, (g + 1) * SSD_GROUP_WIDTH)
            pairs = SSD_HEADS // 2 // SSD_GROUPS
            yg = (jnp.concatenate(y_diag[g * pairs:(g + 1) * pairs], axis=1) + jnp.exp2(csg[:, gc]) * y_off[g]
                  + dskip_ref[:, gc] * xs[:, gc])
            yg = yg * _silu(z_s[rows, gc])
            y_s[rows, GM_WIDTH + g * SSD_GROUP_WIDTH:GM_WIDTH + (g + 1) * SSD_GROUP_WIDTH] = _rms(yg).astype(BF16)
            yield SEG_COST["gate_half"]

        xd = (xdt * jnp.exp2(cs_last - csg)).astype(BF16)
        chunk_decay = jnp.exp2(cs_last)
        for g in range(SSD_GROUPS):
            gc = slice(g * SSD_GROUP_WIDTH, (g + 1) * SSD_GROUP_WIDTH)
            b_t = bm[:, g * SSD_STATE:(g + 1) * SSD_STATE].T.astype(BF16)
            state_s[:, gc] = keep(state_s[:, gc]) * chunk_decay[:, gc] + _dot(b_t, xd[:, gc])
        yield SEG_COST["state"]

    hist_s[...] = xbc_s[ts:ts + SUBLANES, :]

    for h in range(GM_HEADS):
        cols = slice(h * LANES, (h + 1) * LANES)
        vcols = slice(GM_WIDTH + h * LANES, GM_WIDTH + (h + 1) * LANES)
        vn = jnp.concatenate([_rms(uv_s[rows, vcols], gv_ref[:, cols]).astype(BF16) for rows in chunk_rows], axis=1)
        mixed = _dot(wm_s[h], vn)
        for c, rows in enumerate(chunk_rows):
            uv_s[rows, cols] = uv_s[rows, cols] * (mixed[:, c * LANES:(c + 1) * LANES] + bsb_ref[h])
        if h % 2 == 1:
            yield SEG_COST["gmlp_pair"]
    for rows in chunk_rows:
        y_s[rows, 0:GM_WIDTH] = _rms(uv_s[rows, 0:GM_WIDTH]).astype(BF16)
        yield SEG_COST["gmlp_norm"]


def _mixer_kernel(xa_ref, xc_ref, wuv_ref, wz_ref, wxbc_ref, wdt_ref, gv_ref, ws_ref, bsb_ref,
                  convw_ref, convb_ref, dtb_ref, alog_ref, dskip_ref, wout_ref,
                  o_ref,
                  uv0, z0, xbc0, dt0, uv1, z1, xbc1, dt1, y0, y1, wm_s, state_s, hist_s, xn_s, *,
                  tiles_per_seq):
    j = pl.program_id(0)
    ts = SEQ_TILE
    p0, p1 = (uv0, z0, xbc0, dt0), (uv1, z1, xbc1, dt1)

    @pl.when(j == 0)
    def _():
        for ref in (uv1, z1, xbc1, dt1, y0, y1, state_s, hist_s):
            ref[...] = jnp.zeros_like(ref)
        row = lax.broadcasted_iota(jnp.int32, (CHUNK, CHUNK), 0)
        lane = lax.broadcasted_iota(jnp.int32, (CHUNK, CHUNK), 1)
        for h in range(GM_HEADS):
            wm_s[h] = jnp.where(row >= lane, ws_ref[h], 0.0).astype(BF16)

    w_proj = (wuv_ref, wz_ref, wxbc_ref, wdt_ref, dtb_ref, wout_ref)
    w_proc = (gv_ref, bsb_ref, convw_ref, convb_ref, alog_ref, dskip_ref)
    first_even = (2 * j) % tiles_per_seq == 0

    main_total = _process_cost(ts // CHUNK)
    lo, hi = slice(0, ts), slice(ts, 2 * ts)
    _run_interleaved(_process_stage(p1, w_proc, wm_s, y1, state_s, hist_s, None), main_total,
                     _matmul_items(xa_ref.at[lo], xc_ref.at[lo], o_ref.at[lo], y0, w_proj, p0, xn_s))
    _run_interleaved(_process_stage(p0, w_proc, wm_s, y0, state_s, hist_s, first_even), main_total,
                     _matmul_items(xa_ref.at[hi], xc_ref.at[hi], o_ref.at[hi], y1, w_proj, p1, xn_s))


def _ffn_kernel(h_ref, p_ref, gmlp_ref, w1_ref, w2_ref, gple_ref, wgate_ref, wple_ref, gfin_ref, o_ref):
    h = h_ref[...]
    n = _rms(h, gmlp_ref[...]).astype(BF16)
    for j in range(D_FF // FF_BLOCK):
        blk = slice(j * FF_BLOCK, (j + 1) * FF_BLOCK)
        hid = jnp.maximum(_dot(n, w1_ref[:, blk]), 0.0)
        h = h + _dot((hid * hid).astype(BF16), w2_ref[blk, :])
    gate = jax.nn.sigmoid(_dot(_rms(h, gple_ref[...]).astype(BF16), wgate_ref[...]))
    h = h + gate * _dot(p_ref[...].astype(BF16), wple_ref[...])
    o_ref[...] = _rms(h, gfin_ref[...])


def _const_spec(shape):
    nd = len(shape)
    return pl.BlockSpec(shape, lambda *_: (0,) * nd, pipeline_mode=pl.Buffered(1))


def kernel(x, p, norm_mix_g, w_in, gm_v_norm_g, gm_ws, gm_bs, gm_out_norm_g, ssd_conv_w, ssd_conv_b, ssd_dt_bias,
           ssd_a_log, ssd_d, ssd_norm_g, w_out, norm_mlp_g, w_ff1, w_ff2, ple_norm_g, w_ple_gate, w_ple_proj,
           final_norm_g):
    b, s, d = x.shape
    t = b * s
    depth = w_in.shape[0]
    tiles_per_seq = s // SEQ_TILE
    assert depth == 1 and d == D_MODEL and s % (2 * SEQ_TILE) == 0 and t % FFN_TILE == 0
    row2 = lambda v: v.reshape(1, -1).astype(F32)
    i = 0

    w = (norm_mix_g[i].astype(F32)[:, None] * w_in[i].astype(F32)).astype(BF16)
    out_gain = jnp.concatenate([gm_out_norm_g[i], ssd_norm_g[i]]).astype(F32)
    w_o = (out_gain[:, None] * w_out[i].astype(F32)).astype(BF16)
    c0, c1, c2 = 2 * GM_WIDTH, 2 * GM_WIDTH + SSD_WIDTH, 2 * GM_WIDTH + SSD_WIDTH + SSD_CONV_CH
    w_dt = w[:, c2:].T
    dt_bias = jnp.broadcast_to(ssd_dt_bias[i].astype(F32)[:, None], (SSD_HEADS, SEQ_TILE))
    a_log = jnp.broadcast_to(ssd_a_log[i].astype(F32)[:, None], (SSD_HEADS, LANES))
    bsb = jnp.broadcast_to(gm_bs[i].astype(F32)[:, :, None], (GM_HEADS, CHUNK, LANES))
    d_skip = jnp.repeat(ssd_d[i].astype(F32), SSD_HEAD_DIM).reshape(1, SSD_WIDTH)
    mixer_in = [
        w[:, :c0], w[:, c0:c1], w[:, c1:c2], w_dt, row2(gm_v_norm_g[i]), gm_ws[i].astype(F32), bsb,
        ssd_conv_w[i].astype(F32), row2(ssd_conv_b[i]), dt_bias, a_log, d_skip, w_o,
    ]
    nb = t // (2 * SEQ_TILE)
    x2 = x.reshape(t, D_MODEL)
    pbufs = [
        pltpu.VMEM((SEQ_TILE, 2 * GM_WIDTH), F32),
        pltpu.VMEM((SEQ_TILE, SSD_WIDTH), F32),
        pltpu.VMEM((SEQ_TILE + SUBLANES, SSD_CONV_CH), F32),
        pltpu.VMEM((SSD_HEADS, SEQ_TILE), F32),
    ]
    h = pl.pallas_call(
        functools.partial(_mixer_kernel, tiles_per_seq=tiles_per_seq),
        name="mixer",
        grid=(nb + 1,),
        in_specs=[pl.BlockSpec((2 * SEQ_TILE, D_MODEL), lambda j: (jnp.minimum(j, nb - 1), 0)),
                  pl.BlockSpec((2 * SEQ_TILE, D_MODEL), lambda j: (jnp.maximum(j - 1, 0), 0))]
                 + [_const_spec(a.shape) for a in mixer_in],
        out_specs=pl.BlockSpec((2 * SEQ_TILE, D_MODEL), lambda j: (jnp.maximum(j - 1, 0), 0)),
        out_shape=jax.ShapeDtypeStruct((t, D_MODEL), F32),
        scratch_shapes=pbufs + pbufs + [
            pltpu.VMEM((SEQ_TILE, GM_WIDTH + SSD_WIDTH), BF16),
            pltpu.VMEM((SEQ_TILE, GM_WIDTH + SSD_WIDTH), BF16),
            pltpu.VMEM((GM_HEADS, CHUNK, CHUNK), BF16),
            pltpu.VMEM((SSD_STATE, SSD_WIDTH), F32),
            pltpu.VMEM((SUBLANES, SSD_CONV_CH), F32),
            pltpu.VMEM((SEQ_TILE, D_MODEL), BF16),
        ],
        compiler_params=pltpu.CompilerParams(dimension_semantics=("arbitrary",), vmem_limit_bytes=VMEM_LIMIT),
    )(x2, x2, *mixer_in)

    ffn_in = [row2(norm_mlp_g[i]), w_ff1[i].astype(BF16), w_ff2[i].astype(BF16), row2(ple_norm_g[i]),
              w_ple_gate[i].astype(BF16), w_ple_proj[i].astype(BF16), row2(final_norm_g)]
    out = pl.pallas_call(
        _ffn_kernel,
        name="ffn",
        grid=(t // FFN_TILE,),
        in_specs=[pl.BlockSpec((FFN_TILE, D_MODEL), lambda ti: (ti, 0)),
                  pl.BlockSpec((FFN_TILE, D_PLE), lambda ti: (ti, 0))]
                 + [_const_spec(a.shape) for a in ffn_in],
        out_specs=pl.BlockSpec((FFN_TILE, D_MODEL), lambda ti: (ti, 0)),
        out_shape=jax.ShapeDtypeStruct((t, D_MODEL), F32),
        compiler_params=pltpu.CompilerParams(dimension_semantics=("arbitrary",), vmem_limit_bytes=VMEM_LIMIT),
    )(h, p[i].reshape(t, D_PLE), *ffn_in)
    return out.reshape(b, s, D_MODEL)
```

```python
import functools

import numpy as np
import jax
import jax.numpy as jnp
from jax import lax
from jax.experimental import pallas as pl
from jax.experimental.pallas import tpu as pltpu

D_MODEL = 1024
D_PLE = 256
D_FF = 4096
GM_WIDTH = 1024
GM_HEADS = 8
SSD_WIDTH = 1024
SSD_HEADS = 16
SSD_HEAD_DIM = 64
SSD_GROUPS = 2
SSD_GROUP_WIDTH = SSD_WIDTH // SSD_GROUPS
SSD_STATE = 128
SSD_CONV = 4
SSD_CONV_CH = SSD_WIDTH + 2 * SSD_GROUPS * SSD_STATE
CHUNK = 128
EPS = 1e-6
LOG2E = float(np.log2(np.e))

LANES = 128
SUBLANES = 8
SEQ_TILE = 256
FFN_TILE = 1024
FF_BLOCK = 1024
MM_SLAB = 512
OUT_SLAB = 512
CONV_SLAB = 512
LEAD = 0.0
VMEM_LIMIT = 56 * 1024 * 1024
FFN_VMEM_LIMIT = 58 * 1024 * 1024

F32 = jnp.float32
BF16 = jnp.bfloat16


def _dot(a, b):
    return jnp.dot(a, b, preferred_element_type=F32)


def _rms(x, g=None):
    ms = jnp.mean(x * x, axis=-1, keepdims=True)
    y = x * lax.rsqrt(ms + EPS)
    return y if g is None else y * g


def _silu(x):
    h = 0.5 * x
    return h + h * jnp.tanh(h)


SEG_COST = dict(gmlp_pair=(130, 64), gmlp_norm=(190, 0), conv_slab=(230, 0), decay=(60, 64), yoff=(20, 128),
                head_pair=(60, 32), state=(150, 128), gate_half=(175, 0))
ITEM_COST = dict(out_slab=(160, 1024), norm=(280, 0), uv_slab=(250, 512), plain_slab=(64, 512), dt=(30, 300))


def _matmul_items(x_ref, o_ref, y_s, w, pbuf, xn_s):
    wuv_ref, wz_ref, wxbc_ref, wdt_ref, dtb_ref, wout_ref = w
    uv_s, z_s, xbc_s, dt_s = pbuf
    ts = x_ref.shape[0]

    def norm():
        xn_s[...] = _rms(x_ref[...]).astype(BF16)

    def uv_slab(cols):
        def run():
            uv = _dot(xn_s[...], wuv_ref[:, cols])
            uv_s[:, cols] = 0.5 * uv * (1.0 + lax.erf(uv * np.float32(np.sqrt(0.5))))
        return run

    def plain_slab(dst, w_ref, cols, row0):
        def run():
            dst[row0:row0 + ts, cols] = _dot(xn_s[...], w_ref[:, cols])
        return run

    def dt_item():
        dt_raw = lax.dot_general(wdt_ref[...], xn_s[...], (((1,), (1,)), ((), ())),
                                 preferred_element_type=F32) + dtb_ref[...]
        dt_s[...] = jnp.maximum(dt_raw, 0.0) + jnp.log1p(jnp.exp(-jnp.abs(dt_raw)))

    def out_slab(cols):
        def run():
            o_ref[:, cols] = _dot(y_s[...], wout_ref[:, cols])
        return run

    slabs = lambda width: [slice(c, c + MM_SLAB) for c in range(0, width, MM_SLAB)]
    items = [(ITEM_COST["out_slab"], out_slab(slice(c, c + OUT_SLAB))) for c in range(0, D_MODEL, OUT_SLAB)]
    items.append((ITEM_COST["norm"], norm))
    items += [(ITEM_COST["uv_slab"], uv_slab(c)) for c in slabs(2 * GM_WIDTH)]
    items += [(ITEM_COST["plain_slab"], plain_slab(xbc_s, wxbc_ref, c, SUBLANES)) for c in slabs(SSD_CONV_CH)]
    items += [(ITEM_COST["plain_slab"], plain_slab(z_s, wz_ref, c, 0)) for c in slabs(SSD_WIDTH)]
    items.append((ITEM_COST["dt"], dt_item))
    return items


def _run_interleaved(main, main_total, items):
    side_total = sum(c[1] for c, _ in items)
    v_done = side_done = 0.0
    pending = list(items)
    for v, _ in main:
        v_done += v
        while pending and ((side_done + pending[0][0][1] / 2) * main_total[0]
                           <= (v_done + LEAD * main_total[0]) * side_total):
            (_, m_item), fn = pending.pop(0)
            fn()
            side_done += m_item
    for _, fn in pending:
        fn()


def _process_cost(n_chunks):
    per_chunk = ((SSD_CONV_CH // CONV_SLAB) * np.array(SEG_COST["conv_slab"]) + np.array(SEG_COST["decay"])
                 + np.array(SEG_COST["yoff"]) + (SSD_HEADS // 2) * np.array(SEG_COST["head_pair"])
                 + np.array(SEG_COST["state"]) + SSD_GROUPS * np.array(SEG_COST["gate_half"])
                 + np.array(SEG_COST["gmlp_norm"]))
    return tuple((GM_HEADS // 2) * np.array(SEG_COST["gmlp_pair"]) + n_chunks * per_chunk)


def _process_stage(pbuf, w, wm_s, y_s, state_s, hist_s, first):
    gv_ref, bsb_ref, convw_ref, convb_ref, alog_ref, dskip_ref = w
    uv_s, z_s, xbc_s, dt_s = pbuf
    ts = uv_s.shape[0]
    n_chunks = ts // CHUNK
    chunk_rows = [slice(c * CHUNK, (c + 1) * CHUNK) for c in range(n_chunks)]
    carried = (lambda v: v) if first is None else (lambda v: jnp.where(first, 0.0, v))

    row = lax.broadcasted_iota(jnp.int32, (CHUNK, CHUNK), 0)
    lane = lax.broadcasted_iota(jnp.int32, (CHUNK, CHUNK), 1)
    tril = row >= lane
    lo_half = lane < SSD_HEAD_DIM
    a_neg = -jnp.exp(alog_ref[...]) * LOG2E

    xbc_s[0:SUBLANES, :] = carried(hist_s[...])

    conv_out = []
    for c in range(n_chunks):
        r0 = c * CHUNK
        parts = []
        for c0 in range(0, SSD_CONV_CH, CONV_SLAB):
            ccols = slice(c0, c0 + CONV_SLAB)
            blk = xbc_s[r0:r0 + SUBLANES + CHUNK, ccols]
            acc = convb_ref[:, ccols] + convw_ref[SSD_CONV - 1:SSD_CONV, ccols] * blk[SUBLANES:, :]
            for lag in range(1, SSD_CONV):
                k = SSD_CONV - 1 - lag
                acc = acc + convw_ref[k:k + 1, ccols] * pltpu.roll(blk, lag, axis=0)[SUBLANES:, :]
            parts.append(_silu(acc))
            yield SEG_COST["conv_slab"]
        conv_out.append(jnp.concatenate(parts, axis=1))

    for c, rows in enumerate(chunk_rows):
        keep = carried if c == 0 else (lambda v: v)
        xbc = conv_out[c]
        xs = xbc[:, :SSD_WIDTH]
        bm = xbc[:, SSD_WIDTH:SSD_WIDTH + SSD_GROUPS * SSD_STATE]
        cm = xbc[:, SSD_WIDTH + SSD_GROUPS * SSD_STATE:]

        dt_t = dt_s[:, rows]
        cs_t = dt_t * a_neg
        lane16 = lax.broadcasted_iota(jnp.int32, (SSD_HEADS, CHUNK), 1)
        k = 1
        while k < CHUNK:
            cs_t = cs_t + jnp.where(lane16 >= k, pltpu.roll(cs_t, k, axis=1), 0.0)
            k *= 2

        pad = jnp.zeros((CHUNK - 2 * SSD_HEADS, CHUNK), F32)
        cols = jnp.concatenate([cs_t, dt_t, pad], axis=0).T
        cb = []
        for g in range(SSD_GROUPS):
            gs = slice(g * SSD_STATE, (g + 1) * SSD_STATE)
            cb.append(lax.dot_general(cm[:, gs].astype(BF16), bm[:, gs].astype(BF16),
                                      (((1,), (1,)), ((), ())), preferred_element_type=F32))
        yield SEG_COST["decay"]

        y_off = []
        for g in range(SSD_GROUPS):
            gc = slice(g * SSD_GROUP_WIDTH, (g + 1) * SSD_GROUP_WIDTH)
            y_off.append(_dot(cm[:, g * SSD_STATE:(g + 1) * SSD_STATE].astype(BF16),
                              keep(state_s[:, gc]).astype(BF16)))
        yield SEG_COST["yoff"]

        y_diag, csg, xdt = [], [], []
        for j in range(SSD_HEADS // 2):
            g = j // (SSD_HEADS // 2 // SSD_GROUPS)
            r0, r1 = 2 * j, 2 * j + 1
            lhs = []
            for r in (r0, r1):
                seg = cols[:, r:r + 1] - cs_t[r:r + 1, :]
                lhs.append((cb[g] * jnp.exp2(jnp.where(tril, seg, -jnp.inf))).astype(BF16))
            dt_pair = jnp.where(lo_half, cols[:, SSD_HEADS + r0:SSD_HEADS + r0 + 1],
                                cols[:, SSD_HEADS + r1:SSD_HEADS + r1 + 1])
            xp = xs[:, j * LANES:(j + 1) * LANES] * dt_pair
            rhs = jnp.concatenate([jnp.where(lo_half, xp, 0.0), jnp.where(lo_half, 0.0, xp)], axis=0).astype(BF16)
            y_diag.append(_dot(jnp.concatenate(lhs, axis=1), rhs))
            csg.append(jnp.where(lo_half, cols[:, r0:r0 + 1], cols[:, r1:r1 + 1]))
            xdt.append(xp)
            yield SEG_COST["head_pair"]
        csg = jnp.concatenate(csg, axis=1)
        xdt = jnp.concatenate(xdt, axis=1)
        cs_last = csg[CHUNK - 1:CHUNK, :]

        for g in range(SSD_GROUPS):
            gc = slice(g * SSD_GROUP_WIDTH, (g + 1) * SSD_GROUP_WIDTH)
            pairs = SSD_HEADS // 2 // SSD_GROUPS
            yg = (jnp.concatenate(y_diag[g * pairs:(g + 1) * pairs], axis=1) + jnp.exp2(csg[:, gc]) * y_off[g]
                  + dskip_ref[:, gc] * xs[:, gc])
            yg = yg * _silu(z_s[rows, gc])
            y_s[rows, GM_WIDTH + g * SSD_GROUP_WIDTH:GM_WIDTH + (g + 1) * SSD_GROUP_WIDTH] = _rms(yg).astype(BF16)
            yield SEG_COST["gate_half"]

        xd = (xdt * jnp.exp2(cs_last - csg)).astype(BF16)
        chunk_decay = jnp.exp2(cs_last)
        for g in range(SSD_GROUPS):
            gc = slice(g * SSD_GROUP_WIDTH, (g + 1) * SSD_GROUP_WIDTH)
            b_t = bm[:, g * SSD_STATE:(g + 1) * SSD_STATE].T.astype(BF16)
            state_s[:, gc] = keep(state_s[:, gc]) * chunk_decay[:, gc] + _dot(b_t, xd[:, gc])
        yield SEG_COST["state"]

    hist_s[...] = xbc_s[ts:ts + SUBLANES, :]

    for h in range(GM_HEADS):
        cols = slice(h * LANES, (h + 1) * LANES)
        vcols = slice(GM_WIDTH + h * LANES, GM_WIDTH + (h + 1) * LANES)
        vn = jnp.concatenate([_rms(uv_s[rows, vcols], gv_ref[:, cols]).astype(BF16) for rows in chunk_rows], axis=1)
        mixed = _dot(wm_s[h], vn)
        for c, rows in enumerate(chunk_rows):
            uv_s[rows, cols] = uv_s[rows, cols] * (mixed[:, c * LANES:(c + 1) * LANES] + bsb_ref[h])
        if h % 2 == 1:
            yield SEG_COST["gmlp_pair"]
    for rows in chunk_rows:
        y_s[rows, 0:GM_WIDTH] = _rms(uv_s[rows, 0:GM_WIDTH]).astype(BF16)
        yield SEG_COST["gmlp_norm"]


def _mixer_kernel(xa_ref, wuv_ref, wz_ref, wxbc_ref, wdt_ref, gv_ref, ws_ref, bsb_ref,
                  convw_ref, convb_ref, dtb_ref, alog_ref, dskip_ref, wout_ref,
                  o_ref,
                  uv0, z0, xbc0, dt0, uv1, z1, xbc1, dt1, y0, y1, wm_s, state_s, hist_s, xn_s, *,
                  tiles_per_seq):
    j = pl.program_id(0)
    ts = SEQ_TILE
    p0, p1 = (uv0, z0, xbc0, dt0), (uv1, z1, xbc1, dt1)

    @pl.when(j == 0)
    def _():
        for ref in (uv1, z1, xbc1, dt1, y0, y1, state_s, hist_s):
            ref[...] = jnp.zeros_like(ref)
        row = lax.broadcasted_iota(jnp.int32, (CHUNK, CHUNK), 0)
        lane = lax.broadcasted_iota(jnp.int32, (CHUNK, CHUNK), 1)
        for h in range(GM_HEADS):
            wm_s[h] = jnp.where(row >= lane, ws_ref[h], 0.0).astype(BF16)

    w_proj = (wuv_ref, wz_ref, wxbc_ref, wdt_ref, dtb_ref, wout_ref)
    w_proc = (gv_ref, bsb_ref, convw_ref, convb_ref, alog_ref, dskip_ref)
    first_even = (2 * j) % tiles_per_seq == 0

    main_total = _process_cost(ts // CHUNK)
    lo, hi = slice(0, ts), slice(ts, 2 * ts)
    _run_interleaved(_process_stage(p1, w_proc, wm_s, y1, state_s, hist_s, None), main_total,
                     _matmul_items(xa_ref.at[lo], o_ref.at[lo], y0, w_proj, p0, xn_s))
    _run_interleaved(_process_stage(p0, w_proc, wm_s, y0, state_s, hist_s, first_even), main_total,
                     _matmul_items(xa_ref.at[hi], o_ref.at[hi], y1, w_proj, p1, xn_s))


def _ffn_kernel(x_ref, m_ref, p_ref, gmlp_ref, w1_ref, w2_ref, gple_ref, wgate_ref, wple_ref, gfin_ref, o_ref):
    h = x_ref[...] + m_ref[...]
    n = _rms(h, gmlp_ref[...]).astype(BF16)
    for j in range(D_FF // FF_BLOCK):
        blk = slice(j * FF_BLOCK, (j + 1) * FF_BLOCK)
        hid = jnp.maximum(_dot(n, w1_ref[:, blk]), 0.0)
        h = h + _dot((hid * hid).astype(BF16), w2_ref[blk, :])
    gate = jax.nn.sigmoid(_dot(_rms(h, gple_ref[...]).astype(BF16), wgate_ref[...]))
    h = h + gate * _dot(p_ref[...].astype(BF16), wple_ref[...])
    o_ref[...] = _rms(h, gfin_ref[...])


def _const_spec(shape):
    nd = len(shape)
    return pl.BlockSpec(shape, lambda *_: (0,) * nd, pipeline_mode=pl.Buffered(1))


def kernel(x, p, norm_mix_g, w_in, gm_v_norm_g, gm_ws, gm_bs, gm_out_norm_g, ssd_conv_w, ssd_conv_b, ssd_dt_bias,
           ssd_a_log, ssd_d, ssd_norm_g, w_out, norm_mlp_g, w_ff1, w_ff2, ple_norm_g, w_ple_gate, w_ple_proj,
           final_norm_g):
    b, s, d = x.shape
    t = b * s
    depth = w_in.shape[0]
    tiles_per_seq = s // SEQ_TILE
    assert depth == 1 and d == D_MODEL and s % (2 * SEQ_TILE) == 0 and t % FFN_TILE == 0
    row2 = lambda v: v.reshape(1, -1).astype(F32)
    i = 0

    w = (norm_mix_g[i].astype(F32)[:, None] * w_in[i].astype(F32)).astype(BF16)
    out_gain = jnp.concatenate([gm_out_norm_g[i], ssd_norm_g[i]]).astype(F32)
    w_o = (out_gain[:, None] * w_out[i].astype(F32)).astype(BF16)
    c0, c1, c2 = 2 * GM_WIDTH, 2 * GM_WIDTH + SSD_WIDTH, 2 * GM_WIDTH + SSD_WIDTH + SSD_CONV_CH
    w_dt = w[:, c2:].T
    dt_bias = jnp.broadcast_to(ssd_dt_bias[i].astype(F32)[:, None], (SSD_HEADS, SEQ_TILE))
    a_log = jnp.broadcast_to(ssd_a_log[i].astype(F32)[:, None], (SSD_HEADS, LANES))
    bsb = jnp.broadcast_to(gm_bs[i].astype(F32)[:, :, None], (GM_HEADS, CHUNK, LANES))
    d_skip = jnp.repeat(ssd_d[i].astype(F32), SSD_HEAD_DIM).reshape(1, SSD_WIDTH)
    mixer_in = [
        w[:, :c0], w[:, c0:c1], w[:, c1:c2], w_dt, row2(gm_v_norm_g[i]), gm_ws[i].astype(F32), bsb,
        ssd_conv_w[i].astype(F32), row2(ssd_conv_b[i]), dt_bias, a_log, d_skip, w_o,
    ]
    nb = t // (2 * SEQ_TILE)
    x2 = x.reshape(t, D_MODEL)
    pbufs = [
        pltpu.VMEM((SEQ_TILE, 2 * GM_WIDTH), F32),
        pltpu.VMEM((SEQ_TILE, SSD_WIDTH), F32),
        pltpu.VMEM((SEQ_TILE + SUBLANES, SSD_CONV_CH), F32),
        pltpu.VMEM((SSD_HEADS, SEQ_TILE), F32),
    ]
    h = pl.pallas_call(
        functools.partial(_mixer_kernel, tiles_per_seq=tiles_per_seq),
        name="mixer",
        grid=(nb + 1,),
        in_specs=[pl.BlockSpec((2 * SEQ_TILE, D_MODEL), lambda j: (jnp.minimum(j, nb - 1), 0))]
                 + [_const_spec(a.shape) for a in mixer_in],
        out_specs=pl.BlockSpec((2 * SEQ_TILE, D_MODEL), lambda j: (jnp.maximum(j - 1, 0), 0)),
        out_shape=jax.ShapeDtypeStruct((t, D_MODEL), F32),
        scratch_shapes=pbufs + pbufs + [
            pltpu.VMEM((SEQ_TILE, GM_WIDTH + SSD_WIDTH), BF16),
            pltpu.VMEM((SEQ_TILE, GM_WIDTH + SSD_WIDTH), BF16),
            pltpu.VMEM((GM_HEADS, CHUNK, CHUNK), BF16),
            pltpu.VMEM((SSD_STATE, SSD_WIDTH), F32),
            pltpu.VMEM((SUBLANES, SSD_CONV_CH), F32),
            pltpu.VMEM((SEQ_TILE, D_MODEL), BF16),
        ],
        compiler_params=pltpu.CompilerParams(dimension_semantics=("arbitrary",), vmem_limit_bytes=VMEM_LIMIT),
    )(x2, *mixer_in)

    ffn_in = [row2(norm_mlp_g[i]), w_ff1[i].astype(BF16), w_ff2[i].astype(BF16), row2(ple_norm_g[i]),
              w_ple_gate[i].astype(BF16), w_ple_proj[i].astype(BF16), row2(final_norm_g)]
    out = pl.pallas_call(
        _ffn_kernel,
        name="ffn",
        grid=(t // FFN_TILE,),
        in_specs=[pl.BlockSpec((FFN_TILE, D_MODEL), lambda ti: (ti, 0)),
                  pl.BlockSpec((FFN_TILE, D_MODEL), lambda ti: (ti, 0)),
                  pl.BlockSpec((FFN_TILE, D_PLE), lambda ti: (ti, 0))]
                 + [_const_spec(a.shape) for a in ffn_in],
        out_specs=pl.BlockSpec((FFN_TILE, D_MODEL), lambda ti: (ti, 0)),
        out_shape=jax.ShapeDtypeStruct((t, D_MODEL), F32),
        compiler_params=pltpu.CompilerParams(dimension_semantics=("arbitrary",), vmem_limit_bytes=FFN_VMEM_LIMIT),
    )(x2, h, p[i].reshape(t, D_PLE), *ffn_in)
    return out.reshape(b, s, D_MODEL)
```
